```python
import math
import jax
import jax.numpy as jnp
from jax import lax
import numpy as np

D_MODEL = 1024
BATCH = 16
SEQ = 2048
DEPTH = 4

GRID_W = 64
CTX_LEN = 256
HEAD_DIM = 64
DN_HEADS = 4
DN_WIDTH = DN_HEADS * HEAD_DIM
DN_CHUNK = 64
CONV_W = 5
SWA_HEADS = 8
SWA_KV_HEADS = 2
SWA_GROUP = SWA_HEADS // SWA_KV_HEADS
SWA_WIDTH = SWA_HEADS * HEAD_DIM
SWA_KV_WIDTH = SWA_KV_HEADS * HEAD_DIM
WINDOW = 128
SWA_BLOCK = 128
ROPE_BASE = 10000.0
HG_HEADS = 4
HG_WIDTH = HG_HEADS * HEAD_DIM
HG_CHUNK = 64
D_MIX = DN_WIDTH + SWA_WIDTH + HG_WIDTH
D_FF = 4 * D_MODEL
N_MOD = 6
EPS = 1e-6
F32 = jnp.float32
IN_SIZES = (3 * DN_WIDTH, DN_WIDTH, 2 * DN_HEADS, 2 * DN_HEADS,
            SWA_WIDTH, SWA_KV_WIDTH, SWA_KV_WIDTH,
            HG_WIDTH, 2 * HG_WIDTH, HG_WIDTH, HG_WIDTH)
D_IN = 4 * DN_WIDTH + 4 * DN_HEADS + SWA_WIDTH + 2 * SWA_KV_WIDTH + 5 * HG_WIDTH

kernel_name = 'hybrid_dit_deltanet_swa_hgrn2'


def rmsnorm(x, gain):
    xf = x.astype(F32)
    y = xf * lax.rsqrt(jnp.mean(xf * xf, axis=-1, keepdims=True) + EPS)
    return (y * gain.astype(F32)).astype(x.dtype)


def modulate(h, shift, scale):
    return h * (1 + scale) + shift


def to_heads(a, n):
    B, T, _ = a.shape
    return a.reshape(B, T, n, -1).transpose(0, 2, 1, 3)


def from_heads(a):
    B, H, T, d = a.shape
    return a.transpose(0, 2, 1, 3).reshape(B, T, H * d)


def l2norm(a):
    a = a.astype(F32)
    return a * lax.rsqrt(jnp.sum(a * a, axis=-1, keepdims=True) + EPS)


def head_norm_gate(o, gain, gate):
    o = o * lax.rsqrt(jnp.mean(o * o, axis=-1, keepdims=True) + EPS) * gain.astype(F32)
    return (from_heads(o) * jax.nn.silu(gate.astype(F32))).astype(gate.dtype)


def centred_conv(a, w):
    pad = CONV_W // 2
    return lax.conv_general_dilated(a, w[:, None, :], (1,), [(pad, pad)],
                                    dimension_numbers=('NWC', 'WIO', 'NWC'),
                                    feature_group_count=a.shape[-1])


def split_in(p):
    out, start = [], 0
    for size in IN_SIZES:
        out.append(p[..., start:start + size])
        start += size
    return out


def flip_t(a, d):
    return jnp.flip(a, axis=2) if d else a


def to_chunks(a, C):
    B, H, T = a.shape[:3]
    return jnp.moveaxis(a.reshape((B, H, T // C, C) + a.shape[3:]), 2, 0)


def from_chunks(a):
    a = jnp.moveaxis(a, 0, 2)
    return a.reshape(a.shape[:2] + (-1,) + a.shape[4:])


def axial_rope_tables(T):
    rows = T // GRID_W
    row = jnp.repeat(jnp.arange(rows), GRID_W).astype(F32)
    col = jnp.tile(jnp.arange(GRID_W), rows).astype(F32)
    half = HEAD_DIM // 2
    inv = ROPE_BASE ** (-jnp.arange(0, half, 2, dtype=F32) / half)
    ang = jnp.concatenate([row[:, None] * inv, col[:, None] * inv], axis=-1)
    return jnp.cos(ang), jnp.sin(ang)


def apply_rope(a, cos, sin):
    a1, a2 = a[..., :HEAD_DIM // 2], a[..., HEAD_DIM // 2:]
    return jnp.concatenate([a1 * cos - a2 * sin, a1 * sin + a2 * cos], axis=-1)


def gated_delta_scan(q, k, v, beta, g, S0):
    C = DN_CHUNK
    tri = jnp.tril(jnp.ones((C, C), bool))
    strict = jnp.tril(jnp.ones((C, C), bool), -1)
    eye = jnp.eye(C, dtype=F32)

    def step(S, blk):
        qi, ki, vi, bi, gi = blk
        gc = jnp.cumsum(gi, axis=-1)
        decay = jnp.exp(jnp.where(tri, gc[..., :, None] - gc[..., None, :], -jnp.inf))
        kb = ki * bi[..., None]
        a = jnp.where(strict, jnp.einsum('bhid,bhjd->bhij', kb, ki) * decay, 0.0) + eye
        rhs = jnp.concatenate([vi * bi[..., None], kb * jnp.exp(gc)[..., None]], axis=-1)
        uw = lax.linalg.triangular_solve(a, rhs, left_side=True, lower=True, unit_diagonal=True)
        u, w = uw[..., :HEAD_DIM], uw[..., HEAD_DIM:]
        v_new = u - jnp.einsum('bhck,bhkv->bhcv', w, S)
        scores = jnp.einsum('bhid,bhjd->bhij', qi, ki) * decay
        o = (jnp.einsum('bhck,bhkv->bhcv', qi * jnp.exp(gc)[..., None], S)
             + jnp.einsum('bhij,bhjv->bhiv', scores, v_new))
        g_last = gc[..., -1:]
        S = (S * jnp.exp(g_last)[..., None]
             + jnp.einsum('bhck,bhcv->bhkv', ki * jnp.exp(g_last - gc)[..., None], v_new))
        return S, o

    S, o = lax.scan(step, S0, tuple(to_chunks(t, C) for t in (q, k, v, beta, g)))
    return from_chunks(o), S


def gla_scan(q, k, v, logf, S0):
    C = HG_CHUNK
    tri = jnp.tril(jnp.ones((C, C), bool))[:, :, None]

    def step(S, blk):
        qi, ki, vi, lfi = blk
        bc = jnp.cumsum(lfi, axis=2)
        decay = jnp.exp(jnp.where(tri, bc[:, :, :, None, :] - bc[:, :, None, :, :], -jnp.inf))
        scores = jnp.einsum('bhik,bhijk,bhjk->bhij', qi, decay, ki)
        o = (jnp.einsum('bhij,bhjv->bhiv', scores, vi)
             + jnp.einsum('bhck,bhkv->bhcv', qi * jnp.exp(bc), S))
        b_last = bc[:, :, -1:, :]
        S = (S * jnp.exp(b_last)[:, :, 0, :, None]
             + jnp.einsum('bhck,bhcv->bhkv', ki * jnp.exp(b_last - bc), vi))
        return S, o

    S, o = lax.scan(step, S0, tuple(to_chunks(t, C) for t in (q, k, v, logf)))
    return from_chunks(o), S


def deltanet_group(p_lat, p_ctx, conv_w, A_log, dt_bias, norm_g, need_ctx):
    def prep(qkv, a, b):
        qkv = jax.nn.silu(centred_conv(qkv, conv_w))
        q, k, v = jnp.split(qkv, 3, axis=-1)
        q = l2norm(to_heads(q, DN_HEADS)) * HEAD_DIM ** -0.5
        k = l2norm(to_heads(k, DN_HEADS))
        v = to_heads(v, DN_HEADS).astype(F32)
        B, T, _ = a.shape
        a = a.reshape(B, T, 2, DN_HEADS).astype(F32)
        g = -jnp.exp(A_log.astype(F32)) * jax.nn.softplus(a + dt_bias.astype(F32))
        beta = jax.nn.sigmoid(b.reshape(B, T, 2, DN_HEADS).astype(F32))
        return q, k, v, g.transpose(2, 0, 3, 1), beta.transpose(2, 0, 3, 1)

    qkv_l, gate_l, a_l, b_l = p_lat
    qkv_c, gate_c, a_c, b_c = p_ctx
    ql, kl, vl, gl, bl = prep(qkv_l, a_l, b_l)
    qc, kc, vc, gc, bc = prep(qkv_c, a_c, b_c)
    B, H = ql.shape[:2]
    o_l = jnp.zeros_like(vl)
    o_c = jnp.zeros_like(vc)
    for d in range(2):
        S0 = jnp.zeros((B, H, HEAD_DIM, HEAD_DIM), F32)
        oc, S_ctx = gated_delta_scan(flip_t(qc, d), flip_t(kc, d), flip_t(vc, d),
                                     flip_t(bc[d], d), flip_t(gc[d], d), S0)
        ol, _ = gated_delta_scan(flip_t(ql, d), flip_t(kl, d), flip_t(vl, d),
                                 flip_t(bl[d], d), flip_t(gl[d], d), S_ctx)
        o_l = o_l + flip_t(ol, d)
        o_c = o_c + flip_t(oc, d)
    y_l = head_norm_gate(o_l, norm_g, gate_l)
    y_c = head_norm_gate(o_c, norm_g, gate_c) if need_ctx else None
    return y_l, y_c


def swa_group(q_l, k_l, v_l, q_c, k_c, v_c, sink, cos, sin, need_ctx):
    B, T, _ = q_l.shape
    L = k_c.shape[1]
    nb, Bk = T // SWA_BLOCK, SWA_BLOCK
    scale = HEAD_DIM ** -0.5
    ql = apply_rope(to_heads(q_l, SWA_HEADS).astype(F32), cos, sin) * scale
    kl = apply_rope(to_heads(k_l, SWA_KV_HEADS).astype(F32), cos, sin)
    vl = to_heads(v_l, SWA_KV_HEADS).astype(F32)
    kc = to_heads(k_c, SWA_KV_HEADS).astype(F32)
    vc = to_heads(v_c, SWA_KV_HEADS).astype(F32)
    sink = sink.astype(F32).reshape(SWA_KV_HEADS, SWA_GROUP)
    qb = ql.reshape(B, SWA_KV_HEADS, SWA_GROUP, nb, Bk, HEAD_DIM)

    def band(a):
        ap = jnp.pad(a, ((0, 0), (0, 0), (Bk, Bk), (0, 0))).reshape(B, SWA_KV_HEADS, nb + 2, Bk, HEAD_DIM)
        return jnp.concatenate([ap[:, :, :-2], ap[:, :, 1:-1], ap[:, :, 2:]], axis=3)

    kb, vb = band(kl), band(vl)
    blk = jnp.arange(nb)[:, None]
    qpos = blk * Bk + jnp.arange(Bk)[None, :]
    kpos = (blk - 1) * Bk + jnp.arange(3 * Bk)[None, :]
    valid = ((jnp.abs(qpos[:, :, None] - kpos[:, None, :]) <= WINDOW)
             & (kpos[:, None, :] >= 0) & (kpos[:, None, :] < T))
    s_loc = jnp.where(valid, jnp.einsum('bhgnqd,bhnkd->bhgnqk', qb, kb), -jnp.inf)
    s_ctx = jnp.einsum('bhgnqd,bhld->bhgnql', qb, kc)
    s_sink = jnp.broadcast_to(sink[None, :, :, None, None, None], s_loc.shape[:-1] + (1,))
    p = jax.nn.softmax(jnp.concatenate([s_loc, s_ctx, s_sink], axis=-1), axis=-1)
    o = (jnp.einsum('bhgnqk,bhnkd->bhgnqd', p[..., :3 * Bk], vb)
         + jnp.einsum('bhgnql,bhld->bhgnqd', p[..., 3 * Bk:3 * Bk + L], vc))
    y_l = from_heads(o.reshape(B, SWA_HEADS, T, HEAD_DIM)).astype(q_l.dtype)
    y_c = None
    if need_ctx:
        qc = (to_heads(q_c, SWA_HEADS).astype(F32) * scale).reshape(B, SWA_KV_HEADS, SWA_GROUP, L, HEAD_DIM)
        s_cc = jnp.einsum('bhgld,bhmd->bhglm', qc, kc)
        s_sk = jnp.broadcast_to(sink[None, :, :, None, None], s_cc.shape[:-1] + (1,))
        pc = jax.nn.softmax(jnp.concatenate([s_cc, s_sk], axis=-1), axis=-1)
        oc = jnp.einsum('bhglm,bhmd->bhgld', pc[..., :L], vc)
        y_c = from_heads(oc.reshape(B, SWA_HEADS, L, HEAD_DIM)).astype(q_c.dtype)
    return y_l, y_c


def hgrn2_group(p_lat, p_ctx, lb, norm_g, need_ctx):
    lbh = lb.astype(F32).reshape(2, HG_HEADS, HEAD_DIM)

    def prep(q, f, i):
        B, T, _ = f.shape
        z = f.reshape(B, T, 2, HG_HEADS, HEAD_DIM).astype(F32)
        logf = jnp.logaddexp(jnp.log(lbh), jnp.log1p(-lbh) + jax.nn.log_sigmoid(z))
        k = (1 - lbh) * jax.nn.sigmoid(-z)
        perm = (2, 0, 3, 1, 4)
        return (to_heads(q, HG_HEADS).astype(F32), to_heads(i, HG_HEADS).astype(F32),
                k.transpose(perm), logf.transpose(perm))

    q_l, f_l, i_l, gate_l = p_lat
    q_c, f_c, i_c, gate_c = p_ctx
    ql, vl, kl, lfl = prep(q_l, f_l, i_l)
    qc, vc, kc, lfc = prep(q_c, f_c, i_c)
    B = ql.shape[0]
    o_l = jnp.zeros_like(vl)
    o_c = jnp.zeros_like(vc)
    for d in range(2):
        S0 = jnp.zeros((B, HG_HEADS, HEAD_DIM, HEAD_DIM), F32)
        oc, S_ctx = gla_scan(flip_t(qc, d), flip_t(kc[d], d), flip_t(vc, d), flip_t(lfc[d], d), S0)
        ol, _ = gla_scan(flip_t(ql, d), flip_t(kl[d], d), flip_t(vl, d), flip_t(lfl[d], d), S_ctx)
        o_l = o_l + flip_t(ol, d)
        o_c = o_c + flip_t(oc, d)
    y_l = head_norm_gate(o_l, norm_g, gate_l)
    y_c = head_norm_gate(o_c, norm_g, gate_c) if need_ctx else None
    return y_l, y_c


def mixer_layer(hl, hc, w_in, w_out, dn_conv, dn_A_log, dn_dt_bias, dn_norm, swa_sink,
                hg_lb, hg_norm, cos, sin, need_ctx):
    pl = split_in(hl @ w_in)
    pc = split_in(hc @ w_in)
    dn_l, dn_c = deltanet_group(pl[0:4], pc[0:4], dn_conv, dn_A_log, dn_dt_bias, dn_norm, need_ctx)
    sw_l, sw_c = swa_group(pl[4], pl[5], pl[6], pc[4], pc[5], pc[6], swa_sink, cos, sin, need_ctx)
    hg_l, hg_c = hgrn2_group(pl[7:11], pc[7:11], hg_lb, hg_norm, need_ctx)
    yl = jnp.concatenate([dn_l, sw_l, hg_l], axis=-1) @ w_out
    yc = jnp.concatenate([dn_c, sw_c, hg_c], axis=-1) @ w_out if need_ctx else None
    return yl, yc


def sqrelu_mlp(h, w1, w2):
    return jnp.square(jax.nn.relu(h @ w1)) @ w2


def setup_inputs(seed: int = 0) -> dict:
    key = jax.random.key(seed)
    ks = jax.random.split(key, 20)

    def nrm(k, shape, s):
        return jax.random.normal(k, shape, F32) * s

    dt = jnp.exp(jax.random.uniform(ks[11], (DEPTH, 2, DN_HEADS), F32, math.log(1e-3), math.log(1e-1)))
    return {
        'x': nrm(ks[0], (BATCH, SEQ, D_MODEL), 1.0),
        'c': nrm(ks[1], (BATCH, D_MODEL), 1.0),
        'ctx': nrm(ks[2], (BATCH, CTX_LEN, D_MODEL), 1.0),
        'c_ctx': nrm(ks[3], (D_MODEL,), 1.0),
        'w_ada': nrm(ks[4], (DEPTH, D_MODEL, N_MOD * D_MODEL), 0.5 * D_MODEL ** -0.5),
        'b_ada': nrm(ks[5], (DEPTH, N_MOD * D_MODEL), 0.01),
        'norm1': 1.0 + nrm(ks[6], (DEPTH, D_MODEL), 0.02),
        'norm2': 1.0 + nrm(ks[7], (DEPTH, D_MODEL), 0.02),
        'w_in': nrm(ks[8], (DEPTH, D_MODEL, D_IN), D_MODEL ** -0.5),
        'dn_conv': nrm(ks[9], (DEPTH, CONV_W, 3 * DN_WIDTH), CONV_W ** -0.5),
        'dn_A_log': jnp.log(jax.random.uniform(ks[10], (DEPTH, 2, DN_HEADS), F32, 1.0, 16.0)),
        'dn_dt_bias': dt + jnp.log(-jnp.expm1(-dt)),
        'dn_norm': 1.0 + nrm(ks[12], (DEPTH, HEAD_DIM), 0.02),
        'swa_sink': nrm(ks[13], (DEPTH, SWA_HEADS), 0.5),
        'hg_lb_logits': nrm(ks[14], (2, DEPTH, HG_WIDTH), 0.5),
        'hg_norm': 1.0 + nrm(ks[15], (DEPTH, HEAD_DIM), 0.02),
        'w_out': nrm(ks[16], (DEPTH, D_MIX, D_MODEL), D_MIX ** -0.5),
        'w_ff1': nrm(ks[17], (DEPTH, D_MODEL, D_FF), D_MODEL ** -0.5),
        'w_ff2': nrm(ks[18], (DEPTH, D_FF, D_MODEL), D_FF ** -0.5),
        'norm_f': 1.0 + nrm(ks[19], (D_MODEL,), 0.02),
    }


def reference(x, c, ctx, c_ctx, w_ada, b_ada, norm1, norm2, w_in, dn_conv, dn_A_log, dn_dt_bias,
              dn_norm, swa_sink, hg_lb_logits, hg_norm, w_out, w_ff1, w_ff2, norm_f):
    B, T, _ = x.shape
    cos, sin = axial_rope_tables(T)
    lb_all = jnp.cumsum(jax.nn.softmax(hg_lb_logits.astype(F32), axis=1), axis=1)
    lb_all = lb_all - lb_all[:, :1]
    xl, xc = x, ctx
    for l in range(DEPTH):
        need_ctx = l < DEPTH - 1
        ml = (jax.nn.silu(c) @ w_ada[l] + b_ada[l]).reshape(B, N_MOD, 1, D_MODEL)
        mc = (jax.nn.silu(c_ctx) @ w_ada[l] + b_ada[l]).reshape(1, N_MOD, 1, D_MODEL)
        hl = modulate(rmsnorm(xl, norm1[l]), ml[:, 0], ml[:, 1])
        hc = modulate(rmsnorm(xc, norm1[l]), mc[:, 0], mc[:, 1])
        yl, yc = mixer_layer(hl, hc, w_in[l], w_out[l], dn_conv[l], dn_A_log[l], dn_dt_bias[l],
                             dn_norm[l], swa_sink[l], lb_all[:, l], hg_norm[l], cos, sin, need_ctx)
        xl = xl + ml[:, 2] * yl
        xl = xl + ml[:, 5] * sqrelu_mlp(modulate(rmsnorm(xl, norm2[l]), ml[:, 3], ml[:, 4]), w_ff1[l], w_ff2[l])
        if need_ctx:
            xc = xc + mc[:, 2] * yc
            xc = xc + mc[:, 5] * sqrelu_mlp(modulate(rmsnorm(xc, norm2[l]), mc[:, 3], mc[:, 4]), w_ff1[l], w_ff2[l])
    return rmsnorm(xl, norm_f)
```

```python
import functools
import math

import jax
import jax.numpy as jnp
from jax import lax
from jax.experimental import pallas as pl
from jax.experimental.pallas import tpu as pltpu

F32 = jnp.float32
BF16 = jnp.bfloat16

HEAD_DIM = 64
GROUP_LANES = 256
CHUNK = 64
N_MOD = 6
EPS = 1e-6
CONV_W = 5
SWA_HEADS = 8
SWA_KV_HEADS = 2
SWA_BLOCK = 128
WINDOW = 128
GRID_W = 64
ROPE_BASE = 10000.0
HG_SUB = 8
MOD_ROWS = 24
VMEM_LIMIT = 56 * 1024 * 1024
NEG_INF = float("-inf")


def _dot(a, b):
    return jnp.dot(a, b, preferred_element_type=F32)


def _dot_nt(a, b):
    return lax.dot_general(a, b, (((1,), (1,)), ((), ())), preferred_element_type=F32)


def _dot_tn(a, b):
    return lax.dot_general(a, b, (((0,), (0,)), ((), ())), preferred_element_type=F32)


def _split(x):
    hi = x.astype(BF16)
    lo = (x - hi.astype(F32)).astype(BF16)
    return hi, lo


def _iota(shape, dim):
    return lax.broadcasted_iota(jnp.int32, shape, dim)


def _sigmoid(x):
    return 1.0 / (1.0 + jnp.exp(-x))


def _softplus(x):
    return jnp.maximum(x, 0.0) + jnp.log1p(jnp.exp(-jnp.abs(x)))


def _head_block_mask(n):
    return (_iota((n, n), 0) >> 6) == (_iota((n, n), 1) >> 6)


def _mask_bf16(cond):
    return jnp.where(cond, 1.0, 0.0).astype(BF16)


def _blockdiag(x, ones_bd):
    xb = x.astype(BF16)
    return jnp.concatenate([xb, xb, xb, xb], axis=0) * ones_bd


def _chunk_cumsum_mats():
    r = _iota((256, 512), 0)
    c = _iota((256, 512), 1) & 255
    same = (r >> 6) == (c >> 6)
    return _mask_bf16(same & (c <= r)), _mask_bf16(same & (c >= r))


def _chunk_cumsums(x, low, upp):
    hi, lo = _split(x)
    c0 = _dot(low, jnp.concatenate([hi[:, :256], lo[:, :256]], axis=0))
    c1 = _dot(upp, jnp.concatenate([hi[:, 256:], lo[:, 256:]], axis=0))
    return c0, c1


def _backward_chunk(i, n_ctx_chunks, n_chunks):
    return jnp.where(i < n_ctx_chunks, n_ctx_chunks - 1 - i, n_chunks - 1 + n_ctx_chunks - i)


def _ada_kernel(c_ref, w_ref, b_ref, o_ref):
    c = c_ref[...]
    a = c * _sigmoid(c)
    a_hi, a_lo = _split(a)
    w = w_ref[0]
    w_hi, w_lo = _split(w)
    acc = _dot(a_hi, w_hi) + _dot(a_lo, w_hi) + _dot(a_hi, w_lo)
    o_ref[0] = acc + b_ref[0]


def _ada_call(cc, w_ada, b_ada):
    depth, d, n = w_ada.shape
    tn = 1536
    return pl.pallas_call(
        _ada_kernel,
        out_shape=jax.ShapeDtypeStruct((depth, MOD_ROWS, n), F32),
        grid=(depth, n // tn),
        in_specs=[
            pl.BlockSpec((MOD_ROWS, d), lambda l, j: (0, 0)),
            pl.BlockSpec((1, d, tn), lambda l, j: (l, 0, j)),
            pl.BlockSpec((1, 1, tn), lambda l, j: (l, 0, j)),
        ],
        out_specs=pl.BlockSpec((1, MOD_ROWS, tn), lambda l, j: (l, 0, j)),
        compiler_params=pltpu.CompilerParams(
            dimension_semantics=("arbitrary", "arbitrary"), vmem_limit_bytes=VMEM_LIMIT),
        name="ada_mod",
    )(cc, w_ada, b_ada.reshape(depth, 1, n))


def _lb_kernel(x_ref, o_ref):
    depth = x_ref.shape[1]
    rows = [x_ref[:, l, :] for l in range(depth)]
    m = rows[0]
    for r in rows[1:]:
        m = jnp.maximum(m, r)
    es = [jnp.exp(r - m) for r in rows]
    tot = es[0]
    for e in es[1:]:
        tot = tot + e
    ps = [e / tot for e in es]
    run = ps[0]
    o_ref[:, 0, :] = run - ps[0]
    for l in range(1, depth):
        run = run + ps[l]
        o_ref[:, l, :] = run - ps[0]


def _lb_call(logits):
    return pl.pallas_call(
        _lb_kernel,
        out_shape=jax.ShapeDtypeStruct(logits.shape, F32),
        name="hg_lower_bounds",
    )(logits.astype(F32))


def _mod_row(mods_ref, row, idx, d):
    return mods_ref[pl.ds(row, 1), idx * d:(idx + 1) * d]


def _rms(x):
    return x * lax.rsqrt(jnp.mean(x * x, axis=-1, keepdims=True) + EPS)


def _in_proj_kernel(x_ref, mods_ref, g_ref, w_ref, dn_ref, sw_ref, hg_ref, *, ctx_tiles, n_batch):
    d = x_ref.shape[-1]
    row = jnp.where(pl.program_id(1) < ctx_tiles, n_batch, pl.program_id(0))
    shift = _mod_row(mods_ref, row, 0, d)
    scale = _mod_row(mods_ref, row, 1, d)
    h = _rms(x_ref[0]) * g_ref[...] * (1.0 + scale) + shift
    p = _dot(h.astype(BF16), w_ref[...])
    n_dn = dn_ref.shape[-1]
    n_sw = sw_ref.shape[-1]
    dn_ref[0] = p[:, :n_dn]
    sw_ref[0] = p[:, n_dn:n_dn + n_sw]
    hg_ref[0] = p[:, n_dn + n_sw:]


def _in_proj_call(x, mods_l, g1, w_in_p, ctx_len, tm):
    b, s, d = x.shape
    n_dn, n_sw, n_hg = 1152, 768, 1280
    n_all = w_in_p.shape[1]
    kern = functools.partial(_in_proj_kernel, ctx_tiles=ctx_len // tm, n_batch=b)
    return pl.pallas_call(
        kern,
        out_shape=(jax.ShapeDtypeStruct((b, s, n_dn), F32),
                   jax.ShapeDtypeStruct((b, s, n_sw), F32),
                   jax.ShapeDtypeStruct((b, s, n_hg), F32)),
        grid=(b, s // tm),
        in_specs=[
            pl.BlockSpec((1, tm, d), lambda i, t: (i, t, 0)),
            pl.BlockSpec(mods_l.shape, lambda i, t: (0, 0)),
            pl.BlockSpec((1, d), lambda i, t: (0, 0)),
            pl.BlockSpec((d, n_all), lambda i, t: (0, 0), pipeline_mode=pl.Buffered(1)),
        ],
        out_specs=(pl.BlockSpec((1, tm, n_dn), lambda i, t: (i, t, 0)),
                   pl.BlockSpec((1, tm, n_sw), lambda i, t: (i, t, 0)),
                   pl.BlockSpec((1, tm, n_hg), lambda i, t: (i, t, 0))),
        compiler_params=pltpu.CompilerParams(
            dimension_semantics=("arbitrary", "arbitrary"), vmem_limit_bytes=VMEM_LIMIT),
        name="in_proj",
    )(x, mods_l, g1.reshape(1, d), w_in_p)


def _out_mlp_kernel(x_ref, dn_ref, sw_ref, hg_ref, mods_ref, g2_ref, gf_ref, wo_ref, w1_ref, w2_ref, o_ref,
                    *, ctx_tiles, tile_offset, n_batch, final_norm):
    d = x_ref.shape[-1]
    t = pl.program_id(1) + tile_offset
    row = jnp.where(t < ctx_tiles, n_batch, pl.program_id(0))
    y = jnp.concatenate([dn_ref[0], sw_ref[0], hg_ref[0]], axis=-1)
    x1 = x_ref[0] + _mod_row(mods_ref, row, 2, d) * _dot(y, wo_ref[...])
    h = _rms(x1) * g2_ref[...] * (1.0 + _mod_row(mods_ref, row, 4, d)) + _mod_row(mods_ref, row, 3, d)
    hb = h.astype(BF16)
    d_ff = w1_ref.shape[1]
    ff_chunk = 1024
    acc = jnp.zeros_like(x1)
    for j in range(d_ff // ff_chunk):
        a = _dot(hb, w1_ref[:, j * ff_chunk:(j + 1) * ff_chunk])
        a = jnp.square(jnp.maximum(a, 0.0))
        acc = acc + _dot(a.astype(BF16), w2_ref[j * ff_chunk:(j + 1) * ff_chunk, :])
    x2 = x1 + _mod_row(mods_ref, row, 5, d) * acc
    if final_norm:
        x2 = _rms(x2) * gf_ref[...]
    o_ref[0] = x2


def _out_mlp_call(x, y_dn, y_sw, y_hg, mods_l, g2, gf, w_out, w1, w2, ctx_len, tm, latent_only):
    b, s, d = x.shape
    d_ff = w1.shape[1]
    off = ctx_len // tm if latent_only else 0
    s_out = s - ctx_len if latent_only else s
    kern = functools.partial(_out_mlp_kernel, ctx_tiles=ctx_len // tm, tile_offset=off, n_batch=b,
                             final_norm=latent_only)
    tok = lambda w: pl.BlockSpec((1, tm, w), lambda i, t: (i, t + off, 0))
    const = lambda shape: pl.BlockSpec(shape, lambda i, t: (0, 0))
    weight = lambda shape: pl.BlockSpec(shape, lambda i, t: (0, 0), pipeline_mode=pl.Buffered(1))
    return pl.pallas_call(
        kern,
        out_shape=jax.ShapeDtypeStruct((b, s_out, d), F32),
        grid=(b, s_out // tm),
        in_specs=[tok(d), tok(y_dn.shape[-1]), tok(y_sw.shape[-1]), tok(y_hg.shape[-1]),
                  const(mods_l.shape), const((1, d)), const((1, d)),
                  weight((d, d)), weight((d, d_ff)), weight((d_ff, d))],
        out_specs=pl.BlockSpec((1, tm, d), lambda i, t: (i, t, 0)),
        compiler_params=pltpu.CompilerParams(
            dimension_semantics=("arbitrary", "arbitrary"), vmem_limit_bytes=VMEM_LIMIT),
        name="out_mlp",
    )(x, y_dn, y_sw, y_hg, mods_l, g2.reshape(1, d), gf.reshape(1, d), w_out, w1, w2)


def _dn_kernel(p_ref, cw_ref, alog_ref, dtb_ref, ng_ref, o_ref, qkv_s, gc_s, bt_s, os_s, st_s, *, ctx_len):
    s_len = p_ref.shape[1]
    n_chunks = s_len // CHUNK
    n_ctx_chunks = ctx_len // CHUNK
    width = GROUP_LANES
    bdmask = _head_block_mask(width)
    bdones = _mask_bf16(bdmask)

    def conv_tile(t, carry):
        r0 = pl.multiple_of(t * 128, 128)
        top0 = pl.multiple_of(jnp.maximum(r0 - 8, 0), 8)
        bot0 = pl.multiple_of(jnp.minimum(r0 + 128, s_len - 8), 8)
        top = p_ref[0, pl.ds(top0, 8), 0:3 * width]
        mid = p_ref[0, pl.ds(r0, 128), 0:3 * width]
        bot = p_ref[0, pl.ds(bot0, 8), 0:3 * width]
        top_ok = jnp.logical_and(r0 != 0, r0 != ctx_len)
        bot_ok = jnp.logical_and(r0 + 128 != ctx_len, r0 + 128 != s_len)
        top = jnp.where(top_ok, top, 0.0)
        bot = jnp.where(bot_ok, bot, 0.0)
        xp = jnp.concatenate([top, mid, bot], axis=0)
        acc = xp[6:134, :] * cw_ref[0:1, :]
        for j in range(1, CONV_W):
            acc = acc + xp[6 + j:134 + j, :] * cw_ref[j:j + 1, :]
        y = acc * _sigmoid(acc)
        q = y[:, 0:width]
        k = y[:, width:2 * width]
        qs = _dot((q * q).astype(BF16), bdones)
        ks = _dot((k * k).astype(BF16), bdones)
        qkv_s[pl.ds(r0, 128), 0:width] = q * lax.rsqrt(qs + EPS) * (HEAD_DIM ** -0.5)
        qkv_s[pl.ds(r0, 128), width:2 * width] = k * lax.rsqrt(ks + EPS)
        qkv_s[pl.ds(r0, 128), 2 * width:3 * width] = y[:, 2 * width:3 * width]
        return carry

    lax.fori_loop(0, s_len // 128, conv_tile, 0)

    low, upp = _chunk_cumsum_mats()
    expand = _mask_bf16((_iota((256, 1024), 1) >> 6) == (_iota((256, 1024), 0) & 127))

    def gate_tile(t, carry):
        r0 = pl.multiple_of(t * 256, 256)
        ab = p_ref[0, pl.ds(r0, 256), 4 * width:4 * width + 128]
        hi, lo = _split(ab)
        abx = _dot(jnp.concatenate([hi, lo], axis=1), expand)
        g = -jnp.exp(alog_ref[...]) * _softplus(abx[:, 0:2 * width] + dtb_ref[...])
        bt_s[pl.ds(r0, 256), :] = _sigmoid(abx[:, 2 * width:4 * width])
        c0, c1 = _chunk_cumsums(g, low, upp)
        gc_s[pl.ds(r0, 256), 0:width] = c0
        gc_s[pl.ds(r0, 256), width:2 * width] = c1
        os_s[pl.ds(r0, 256), :] = jnp.zeros((256, width), F32)
        return carry

    lax.fori_loop(0, s_len // 256, gate_tile, 0)

    st_s[...] = jnp.zeros(st_s.shape, F32)
    row = _iota((CHUNK, width), 0)
    col = _iota((CHUNK, width), 1) & 63
    eye = jnp.where(row == col, 1.0, 0.0)

    def chunk_dir(d, c):
        r0 = pl.multiple_of(c * CHUNK, CHUNK)
        q = qkv_s[pl.ds(r0, CHUNK), 0:width]
        k = qkv_s[pl.ds(r0, CHUNK), width:2 * width]
        v = qkv_s[pl.ds(r0, CHUNK), 2 * width:3 * width]
        bt = bt_s[pl.ds(r0, CHUNK), d * width:(d + 1) * width]
        gcb = gc_s[pl.ds(r0, CHUNK), d * width:(d + 1) * width]
        if d == 0:
            incl, strict = col <= row, col < row
            gtot = gcb[CHUNK - 1:CHUNK, :]
        else:
            incl, strict = col >= row, col > row
            gtot = gcb[0:1, :]
        grow = jnp.sum(gcb * eye, axis=0, keepdims=True)
        decay = jnp.exp(jnp.where(incl, gcb - grow, NEG_INF))
        kb = k * bt
        egc = jnp.exp(gcb)
        kq = jnp.concatenate([kb, q], axis=0).astype(BF16)
        raw = _dot_nt(kq, _blockdiag(k, bdones))
        a = jnp.where(strict, raw[0:CHUNK] * decay, 0.0)
        sc = raw[CHUNK:2 * CHUNK] * decay
        p = eye - jnp.where((row >> 1) == (col >> 1), a, 0.0)
        for lg in range(1, 6):
            coupling = jnp.where(((row >> (lg + 1)) == (col >> (lg + 1))) & ((row >> lg) != (col >> lg)), a, 0.0)
            ct = _dot(coupling.astype(BF16), _blockdiag(p, bdones))
            p = p - _dot(p.astype(BF16), _blockdiag(ct, bdones))
        rhs = jnp.concatenate([_blockdiag(v * bt, bdones), _blockdiag(kb * egc, bdones)], axis=1)
        uw = _dot(p.astype(BF16), rhs)
        u = uw[:, 0:width]
        w = uw[:, width:2 * width]
        state = st_s[d]
        ws_qs = _dot(jnp.concatenate([w, q * egc], axis=0).astype(BF16), state.astype(BF16))
        v_new = u - ws_qs[0:CHUNK]
        o = ws_qs[CHUNK:2 * CHUNK] + _dot(sc.astype(BF16), _blockdiag(v_new, bdones))
        k_dec = (k * jnp.exp(gtot - gcb)).astype(BF16)
        upd = _dot_tn(k_dec, v_new.astype(BF16))
        st_s[d] = state * jnp.exp(gtot) + jnp.where(bdmask, upd, 0.0)
        os_s[pl.ds(r0, CHUNK), :] = os_s[pl.ds(r0, CHUNK), :] + o

    def scan_step(i, carry):
        chunk_dir(0, i)
        chunk_dir(1, _backward_chunk(i, n_ctx_chunks, n_chunks))
        return carry

    lax.fori_loop(0, n_chunks, scan_step, 0)

    def out_tile(t, carry):
        r0 = pl.multiple_of(t * 256, 256)
        o = os_s[pl.ds(r0, 256), :]
        ms = _dot((o * o).astype(BF16), bdones) * (1.0 / HEAD_DIM)
        gate = p_ref[0, pl.ds(r0, 256), 3 * width:4 * width]
        y = o * lax.rsqrt(ms + EPS) * ng_ref[...] * (gate * _sigmoid(gate))
        o_ref[0, pl.ds(r0, 256), :] = y.astype(o_ref.dtype)
        return carry

    lax.fori_loop(0, s_len // 256, out_tile, 0)


def _dn_call(p_dn, conv_w, a_log, dt_bias, norm_g, ctx_len):
    b, s, n = p_dn.shape
    width = GROUP_LANES
    lanes = lambda a: jnp.repeat(a.astype(F32).reshape(-1), HEAD_DIM).reshape(1, 2 * width)
    ng = jnp.tile(norm_g.astype(F32), width // HEAD_DIM).reshape(1, width)
    const = lambda shape: pl.BlockSpec(shape, lambda i: (0, 0))
    return pl.pallas_call(
        functools.partial(_dn_kernel, ctx_len=ctx_len),
        out_shape=jax.ShapeDtypeStruct((b, s, width), BF16),
        grid=(b,),
        in_specs=[pl.BlockSpec((1, s, n), lambda i: (i, 0, 0)),
                  const((CONV_W, 3 * width)), const((1, 2 * width)), const((1, 2 * width)), const((1, width))],
        out_specs=pl.BlockSpec((1, s, width), lambda i: (i, 0, 0)),
        scratch_shapes=[pltpu.VMEM((s, 3 * width), F32), pltpu.VMEM((s, 2 * width), F32),
                        pltpu.VMEM((s, 2 * width), F32), pltpu.VMEM((s, width), F32),
                        pltpu.VMEM((2, width, width), F32)],
        compiler_params=pltpu.CompilerParams(dimension_semantics=("arbitrary",), vmem_limit_bytes=VMEM_LIMIT),
        name="deltanet",
    )(p_dn, conv_w.astype(F32), lanes(a_log), lanes(dt_bias), ng)


def _hg_kernel(p_ref, lb_ref, ng_ref, o_ref, bc_s, kk_s, os_s, st_s, *, ctx_len):
    s_len = p_ref.shape[1]
    n_chunks = s_len // CHUNK
    n_ctx_chunks = ctx_len // CHUNK
    width = GROUP_LANES
    n_sub = CHUNK // HG_SUB
    bdmask = _head_block_mask(width)
    bdones = _mask_bf16(bdmask)
    low, upp = _chunk_cumsum_mats()
    lb = jnp.concatenate([lb_ref[0:1, :], lb_ref[1:2, :]], axis=1)
    log_lb = jnp.log(lb)
    log_1m = jnp.log1p(-lb)

    def gate_tile(t, carry):
        r0 = pl.multiple_of(t * 256, 256)
        z = p_ref[0, pl.ds(r0, 256), width:3 * width]
        ez = jnp.exp(-jnp.abs(z))
        l1p = jnp.log1p(ez)
        tt = log_1m + (jnp.minimum(z, 0.0) - l1p)
        logf = jnp.maximum(log_lb, tt) + jnp.log1p(jnp.exp(-jnp.abs(log_lb - tt)))
        kk_s[pl.ds(r0, 256), :] = (1.0 - lb) * _sigmoid(-z)
        c0, c1 = _chunk_cumsums(logf, low, upp)
        bc_s[pl.ds(r0, 256), 0:width] = c0
        bc_s[pl.ds(r0, 256), width:2 * width] = c1
        os_s[pl.ds(r0, 256), :] = jnp.zeros((256, width), F32)
        return carry

    lax.fori_loop(0, s_len // 256, gate_tile, 0)

    st_s[...] = jnp.zeros(st_s.shape, F32)
    row = _iota((CHUNK, width), 0)
    col = _iota((CHUNK, width), 1) & 63
    row_sub = row >> 3
    col_sub = col >> 3

    def chunk_dir(d, c):
        r0 = pl.multiple_of(c * CHUNK, CHUNK)
        q = p_ref[0, pl.ds(r0, CHUNK), 0:width]
        v = p_ref[0, pl.ds(r0, CHUNK), 3 * width:4 * width]
        k = kk_s[pl.ds(r0, CHUNK), d * width:(d + 1) * width]
        bcb = bc_s[pl.ds(r0, CHUNK), d * width:(d + 1) * width]
        if d == 0:
            edge = lambda j: bcb[HG_SUB * j + HG_SUB - 1:HG_SUB * j + HG_SUB, :]
            subs = range(0, n_sub - 1)
            b_last = bcb[CHUNK - 1:CHUNK, :]
        else:
            edge = lambda j: bcb[HG_SUB * j:HG_SUB * j + 1, :]
            subs = range(1, n_sub)
            b_last = bcb[0:1, :]
        key_edge = jnp.concatenate([jnp.broadcast_to(edge(j), (HG_SUB, width)) for j in range(n_sub)], axis=0)
        k_til = k * jnp.exp(jnp.minimum(key_edge - bcb, 0.0))
        kbd = _blockdiag(k_til, bdones)
        q_stack = jnp.concatenate(
            [(q * jnp.exp(jnp.minimum(bcb - edge(j), 0.0))).astype(BF16) for j in subs], axis=0)
        off = _dot_nt(q_stack, kbd)
        scores = jnp.zeros((CHUNK, width), F32)
        for n, j in enumerate(subs):
            sel = (col_sub == j) & ((row_sub > j) if d == 0 else (row_sub < j))
            scores = jnp.where(sel, off[n * CHUNK:(n + 1) * CHUNK], scores)
        terms = []
        for delta in range(HG_SUB):
            shift = delta if d == 0 else (CHUNK - delta) % CHUNK
            if delta == 0:
                terms.append((q * k).astype(BF16))
            else:
                ks = pltpu.roll(k, shift, 0)
                bs = pltpu.roll(bcb, shift, 0)
                terms.append((q * ks * jnp.exp(jnp.minimum(bcb - bs, 0.0))).astype(BF16))
        red = _dot(jnp.concatenate(terms, axis=0), bdones)
        for delta in range(HG_SUB):
            if d == 0:
                sel = (col == row - delta) & ((row & (HG_SUB - 1)) >= delta)
            else:
                sel = (col == row + delta) & ((row & (HG_SUB - 1)) < HG_SUB - delta)
            scores = jnp.where(sel, red[delta * CHUNK:(delta + 1) * CHUNK], scores)
        state_t = st_s[d]
        o = _dot(scores.astype(BF16), _blockdiag(v, bdones)) + _dot_nt(
            (q * jnp.exp(bcb)).astype(BF16), state_t.astype(BF16))
        k_dec = (k * jnp.exp(b_last - bcb)).astype(BF16)
        upd = _dot_tn(v.astype(BF16), k_dec)
        st_s[d] = state_t * jnp.exp(b_last) + jnp.where(bdmask, upd, 0.0)
        os_s[pl.ds(r0, CHUNK), :] = os_s[pl.ds(r0, CHUNK), :] + o

    def scan_step(i, carry):
        chunk_dir(0, i)
        chunk_dir(1, _backward_chunk(i, n_ctx_chunks, n_chunks))
        return carry

    lax.fori_loop(0, n_chunks, scan_step, 0)

    def out_tile(t, carry):
        r0 = pl.multiple_of(t * 256, 256)
        o = os_s[pl.ds(r0, 256), :]
        ms = _dot((o * o).astype(BF16), bdones) * (1.0 / HEAD_DIM)
        gate = p_ref[0, pl.ds(r0, 256), 4 * width:5 * width]
        y = o * lax.rsqrt(ms + EPS) * ng_ref[...] * (gate * _sigmoid(gate))
        o_ref[0, pl.ds(r0, 256), :] = y.astype(o_ref.dtype)
        return carry

    lax.fori_loop(0, s_len // 256, out_tile, 0)


def _hg_call(p_hg, lb_l, norm_g, ctx_len):
    b, s, n = p_hg.shape
    width = GROUP_LANES
    ng = jnp.tile(norm_g.astype(F32), width // HEAD_DIM).reshape(1, width)
    const = lambda shape: pl.BlockSpec(shape, lambda i: (0, 0))
    return pl.pallas_call(
        functools.partial(_hg_kernel, ctx_len=ctx_len),
        out_shape=jax.ShapeDtypeStruct((b, s, width), BF16),
        grid=(b,),
        in_specs=[pl.BlockSpec((1, s, n), lambda i: (i, 0, 0)), const((2, width)), const((1, width))],
        out_specs=pl.BlockSpec((1, s, width), lambda i: (i, 0, 0)),
        scratch_shapes=[pltpu.VMEM((s, 2 * width), F32), pltpu.VMEM((s, 2 * width), F32),
                        pltpu.VMEM((s, width), F32), pltpu.VMEM((2, width, width), F32)],
        compiler_params=pltpu.CompilerParams(dimension_semantics=("arbitrary",), vmem_limit_bytes=VMEM_LIMIT),
        name="hgrn2",
    )(p_hg, lb_l, ng)


def _swa_kernel(sink_ref, p_ref, cos_ref, sin_ref, o_ref, q_s, k_s, v_s, *, ctx_len):
    s_len = p_ref.shape[1]
    t_len = s_len - ctx_len
    blk = SWA_BLOCK
    qw = SWA_HEADS * HEAD_DIM
    kvw = SWA_KV_HEADS * HEAD_DIM
    group = SWA_HEADS // SWA_KV_HEADS
    gw = group * HEAD_DIM
    scale = HEAD_DIM ** -0.5
    half = HEAD_DIM // 2
    lane128 = _iota((blk, kvw), 1)
    first_half_q = (_iota((blk, qw), 1) & (HEAD_DIM - 1)) < half
    first_half_k = (lane128 & (HEAD_DIM - 1)) < half
    low_head = lane128 < HEAD_DIM

    def kv_tiles(x):
        sw = pltpu.roll(x, HEAD_DIM, 1)
        return jnp.where(low_head, x, sw), jnp.where(low_head, sw, x)

    def prep_tile(t, carry):
        r0 = pl.multiple_of(t * blk, blk)
        q = p_ref[0, pl.ds(r0, blk), 0:qw]
        k = p_ref[0, pl.ds(r0, blk), qw:qw + kvw]
        v = p_ref[0, pl.ds(r0, blk), qw + kvw:qw + 2 * kvw]
        pos = pl.multiple_of(jnp.maximum(r0 - ctx_len, 0), blk)
        cs = cos_ref[pl.ds(pos, blk), :]
        sn = sin_ref[pl.ds(pos, blk), :]
        is_ctx = r0 < ctx_len
        cs = jnp.where(is_ctx, 1.0, cs)
        sn = jnp.where(is_ctx, 0.0, sn)
        q_sw = jnp.where(first_half_q, pltpu.roll(q, qw - half, 1), pltpu.roll(q, half, 1))
        k_sw = jnp.where(first_half_k, pltpu.roll(k, kvw - half, 1), pltpu.roll(k, half, 1))
        cs4 = jnp.concatenate([cs] * (qw // kvw), axis=1)
        sn4 = jnp.concatenate([sn] * (qw // kvw), axis=1)
        q_s[pl.ds(r0, blk), :] = ((q * cs4 + q_sw * sn4) * scale).astype(BF16)
        k0, k1 = kv_tiles(k * cs + k_sw * sn)
        v0, v1 = kv_tiles(v)
        k_s[0, pl.ds(r0, blk), :] = k0.astype(BF16)
        k_s[1, pl.ds(r0, blk), :] = k1.astype(BF16)
        v_s[0, pl.ds(r0, blk), :] = v0.astype(BF16)
        v_s[1, pl.ds(r0, blk), :] = v1.astype(BF16)
        return carry

    lax.fori_loop(0, s_len // blk, prep_tile, 0)

    rows = group * blk
    blk_shift = blk.bit_length() - 1
    qmask = _mask_bf16((_iota((rows, gw), 0) >> blk_shift) == (_iota((rows, gw), 1) >> 6))
    rb1 = _iota((rows, 1), 0) >> blk_shift
    r_in = _iota((rows, 3 * blk), 0) & (blk - 1)
    c_in = _iota((rows, 3 * blk), 1)
    lane_g = _iota((blk, gw), 1) >> 6

    def wide(x):
        return jnp.concatenate([x, x], axis=1)

    def attend(r0, g, local_start):
        qb = q_s[pl.ds(r0, blk), g * gw:(g + 1) * gw]
        qst = jnp.concatenate([qb] * group, axis=0) * qmask
        sink = jnp.zeros((rows, 1), F32)
        for j in range(group):
            sink = jnp.where(rb1 == j, sink_ref[g * group + j], sink)
        kc = wide(k_s[g, 0:ctx_len, :])
        vc = wide(v_s[g, 0:ctx_len, :])
        s_ctx = _dot_nt(qst, kc)
        m = jnp.maximum(jnp.max(s_ctx, axis=-1, keepdims=True), sink)
        if local_start is not None:
            kl = wide(k_s[g, pl.ds(ctx_len + local_start, 3 * blk), :])
            vl = wide(v_s[g, pl.ds(ctx_len + local_start, 3 * blk), :])
            qpos = (r0 - ctx_len) + r_in
            kpos = local_start + c_in
            valid = jnp.abs(qpos - kpos) <= WINDOW
            s_loc = jnp.where(valid, _dot_nt(qst, kl), NEG_INF)
            m = jnp.maximum(m, jnp.max(s_loc, axis=-1, keepdims=True))
            e_loc = jnp.exp(s_loc - m)
        e_ctx = jnp.exp(s_ctx - m)
        den = jnp.sum(e_ctx, axis=-1, keepdims=True) + jnp.exp(sink - m)
        acc = _dot(e_ctx.astype(BF16), vc)
        if local_start is not None:
            den = den + jnp.sum(e_loc, axis=-1, keepdims=True)
            acc = acc + _dot(e_loc.astype(BF16), vl)
        acc = acc / den
        out = jnp.zeros((blk, gw), F32)
        for j in range(group):
            out = jnp.where(lane_g == j, acc[j * blk:(j + 1) * blk], out)
        o_ref[0, pl.ds(r0, blk), g * gw:(g + 1) * gw] = out.astype(o_ref.dtype)

    def ctx_block(t, carry):
        r0 = pl.multiple_of(t * blk, blk)
        for g in range(SWA_KV_HEADS):
            attend(r0, g, None)
        return carry

    lax.fori_loop(0, ctx_len // blk, ctx_block, 0)

    def lat_block(t, carry):
        r0 = pl.multiple_of(ctx_len + t * blk, blk)
        start = pl.multiple_of(jnp.clip((t - 1) * blk, 0, t_len - 3 * blk), blk)
        for g in range(SWA_KV_HEADS):
            attend(r0, g, start)
        return carry

    lax.fori_loop(0, t_len // blk, lat_block, 0)


def _swa_call(p_sw, sink, cos_t, sin_t, ctx_len):
    b, s, n = p_sw.shape
    qw = SWA_HEADS * HEAD_DIM
    kvw = SWA_KV_HEADS * HEAD_DIM
    t_len = s - ctx_len
    return pl.pallas_call(
        functools.partial(_swa_kernel, ctx_len=ctx_len),
        out_shape=jax.ShapeDtypeStruct((b, s, qw), BF16),
        grid=(b,),
        in_specs=[pl.BlockSpec(memory_space=pltpu.SMEM),
                  pl.BlockSpec((1, s, n), lambda i: (i, 0, 0)),
                  pl.BlockSpec((t_len, kvw), lambda i: (0, 0)),
                  pl.BlockSpec((t_len, kvw), lambda i: (0, 0))],
        out_specs=pl.BlockSpec((1, s, qw), lambda i: (i, 0, 0)),
        scratch_shapes=[pltpu.VMEM((s, qw), BF16), pltpu.VMEM((SWA_KV_HEADS, s, kvw), BF16),
                        pltpu.VMEM((SWA_KV_HEADS, s, kvw), BF16)],
        compiler_params=pltpu.CompilerParams(dimension_semantics=("arbitrary",), vmem_limit_bytes=VMEM_LIMIT),
        name="swa",
    )(sink.astype(F32), p_sw, cos_t, sin_t)


def _rope_tables(t_len):
    rows = t_len // GRID_W
    row = jnp.repeat(jnp.arange(rows), GRID_W).astype(F32)
    col = jnp.tile(jnp.arange(GRID_W), rows).astype(F32)
    half = HEAD_DIM // 2
    inv = ROPE_BASE ** (-jnp.arange(0, half, 2, dtype=F32) / half)
    ang = jnp.concatenate([row[:, None] * inv, col[:, None] * inv], axis=-1)
    cos, sin = jnp.cos(ang), jnp.sin(ang)
    return jnp.tile(cos, (1, 4)), jnp.tile(jnp.concatenate([-sin, sin], axis=-1), (1, 2))


def kernel(x, c, ctx, c_ctx, w_ada, b_ada, norm1, norm2, w_in, dn_conv, dn_A_log, dn_dt_bias, dn_norm, swa_sink,
           hg_lb_logits, hg_norm, w_out, w_ff1, w_ff2, norm_f):
    b, t_len, d = x.shape
    ctx_len = ctx.shape[1]
    depth = w_ada.shape[0]
    tm = 256
    assert b + 1 <= MOD_ROWS and ctx_len % tm == 0 and t_len % tm == 0 and t_len >= 3 * SWA_BLOCK

    cc = jnp.zeros((MOD_ROWS, d), F32).at[:b].set(c.astype(F32)).at[b].set(c_ctx.astype(F32))
    mods = _ada_call(cc, w_ada.astype(F32), b_ada.astype(F32))
    lb_all = _lb_call(hg_lb_logits)
    cos_t, sin_t = _rope_tables(t_len)

    n_head = 4 * GROUP_LANES + 16
    w_in_p = jnp.concatenate(
        [w_in[:, :, :n_head], jnp.zeros((depth, d, 128 - 16), w_in.dtype), w_in[:, :, n_head:]], axis=-1).astype(BF16)
    w_out_b = w_out.astype(BF16)
    w1_b = w_ff1.astype(BF16)
    w2_b = w_ff2.astype(BF16)

    xs = jnp.concatenate([ctx.astype(F32), x.astype(F32)], axis=1)
    for l in range(depth):
        last = l == depth - 1
        p_dn, p_sw, p_hg = _in_proj_call(xs, mods[l], norm1[l].astype(F32), w_in_p[l], ctx_len, tm)
        y_dn = _dn_call(p_dn, dn_conv[l], dn_A_log[l], dn_dt_bias[l], dn_norm[l], ctx_len)
        y_sw = _swa_call(p_sw, swa_sink[l], cos_t, sin_t, ctx_len)
        y_hg = _hg_call(p_hg, lb_all[:, l], hg_norm[l], ctx_len)
        xs = _out_mlp_call(xs, y_dn, y_sw, y_hg, mods[l], norm2[l].astype(F32), norm_f.astype(F32),
                           w_out_b[l], w1_b[l], w2_b[l], ctx_len, tm, latent_only=last)
    return xs.astype(x.dtype)
```

```python
import functools
import math

import jax
import jax.numpy as jnp
from jax import lax
from jax.experimental import pallas as pl
from jax.experimental.pallas import tpu as pltpu

F32 = jnp.float32
BF16 = jnp.bfloat16

HEAD_DIM = 64
GROUP_LANES = 256
CHUNK = 64
N_MOD = 6
EPS = 1e-6
CONV_W = 5
SWA_HEADS = 8
SWA_KV_HEADS = 2
SWA_BLOCK = 128
WINDOW = 128
GRID_W = 64
ROPE_BASE = 10000.0
HG_SUB = 8
MOD_ROWS = 24
VMEM_LIMIT = 56 * 1024 * 1024
NEG_INF = float("-inf")


def _dot(a, b):
    return jnp.dot(a, b, preferred_element_type=F32)


def _dot_nt(a, b):
    return lax.dot_general(a, b, (((1,), (1,)), ((), ())), preferred_element_type=F32)


def _dot_tn(a, b):
    return lax.dot_general(a, b, (((0,), (0,)), ((), ())), preferred_element_type=F32)


def _split(x):
    hi = x.astype(BF16)
    lo = (x - hi.astype(F32)).astype(BF16)
    return hi, lo


def _iota(shape, dim):
    return lax.broadcasted_iota(jnp.int32, shape, dim)


def _sigmoid(x):
    return 1.0 / (1.0 + jnp.exp(-x))


def _softplus(x):
    return jnp.maximum(x, 0.0) + jnp.log1p(jnp.exp(-jnp.abs(x)))


def _head_block_mask(n):
    return (_iota((n, n), 0) >> 6) == (_iota((n, n), 1) >> 6)


def _mask_bf16(cond):
    return jnp.where(cond, 1.0, 0.0).astype(BF16)


def _blockdiag(x, ones_bd):
    xb = x.astype(BF16)
    return jnp.concatenate([xb, xb, xb, xb], axis=0) * ones_bd


def _chunk_cumsum_mats():
    r = _iota((256, 512), 0)
    c = _iota((256, 512), 1) & 255
    same = (r >> 6) == (c >> 6)
    return _mask_bf16(same & (c <= r)), _mask_bf16(same & (c >= r))


def _chunk_cumsums(x, low, upp):
    hi, lo = _split(x)
    c0 = _dot(low, jnp.concatenate([hi[:, :256], lo[:, :256]], axis=0))
    c1 = _dot(upp, jnp.concatenate([hi[:, 256:], lo[:, 256:]], axis=0))
    return c0, c1


def _backward_chunk(i, n_ctx_chunks, n_chunks):
    return jnp.where(i < n_ctx_chunks, n_ctx_chunks - 1 - i, n_chunks - 1 + n_ctx_chunks - i)


def _ada_kernel(c_ref, w_ref, b_ref, o_ref):
    c = c_ref[...]
    a = c * _sigmoid(c)
    a_hi, a_lo = _split(a)
    w = w_ref[0]
    w_hi, w_lo = _split(w)
    acc = _dot(a_hi, w_hi) + _dot(a_lo, w_hi) + _dot(a_hi, w_lo)
    o_ref[0] = acc + b_ref[0]


def _ada_call(cc, w_ada, b_ada):
    depth, d, n = w_ada.shape
    tn = 1536
    return pl.pallas_call(
        _ada_kernel,
        out_shape=jax.ShapeDtypeStruct((depth, MOD_ROWS, n), F32),
        grid=(depth, n // tn),
        in_specs=[
            pl.BlockSpec((MOD_ROWS, d), lambda l, j: (0, 0)),
            pl.BlockSpec((1, d, tn), lambda l, j: (l, 0, j)),
            pl.BlockSpec((1, 1, tn), lambda l, j: (l, 0, j)),
        ],
        out_specs=pl.BlockSpec((1, MOD_ROWS, tn), lambda l, j: (l, 0, j)),
        compiler_params=pltpu.CompilerParams(
            dimension_semantics=("arbitrary", "arbitrary"), vmem_limit_bytes=VMEM_LIMIT),
        name="ada_mod",
    )(cc, w_ada, b_ada.reshape(depth, 1, n))


def _lb_kernel(x_ref, o_ref):
    depth = x_ref.shape[1]
    rows = [x_ref[:, l, :] for l in range(depth)]
    m = rows[0]
    for r in rows[1:]:
        m = jnp.maximum(m, r)
    es = [jnp.exp(r - m) for r in rows]
    tot = es[0]
    for e in es[1:]:
        tot = tot + e
    ps = [e / tot for e in es]
    run = ps[0]
    o_ref[:, 0, :] = run - ps[0]
    for l in range(1, depth):
        run = run + ps[l]
        o_ref[:, l, :] = run - ps[0]


def _lb_call(logits):
    return pl.pallas_call(
        _lb_kernel,
        out_shape=jax.ShapeDtypeStruct(logits.shape, F32),
        name="hg_lower_bounds",
    )(logits.astype(F32))


def _mod_row(mods_ref, row, idx, d):
    return mods_ref[pl.ds(row, 1), idx * d:(idx + 1) * d]


def _rms(x):
    return x * lax.rsqrt(jnp.mean(x * x, axis=-1, keepdims=True) + EPS)


def _in_proj_kernel(x_ref, mods_ref, g_ref, w_ref, dn_ref, sw_ref, hg_ref, *, ctx_tiles, n_batch):
    d = x_ref.shape[-1]
    row = jnp.where(pl.program_id(1) < ctx_tiles, n_batch, pl.program_id(0))
    shift = _mod_row(mods_ref, row, 0, d)
    scale = _mod_row(mods_ref, row, 1, d)
    h = _rms(x_ref[0]) * g_ref[...] * (1.0 + scale) + shift
    p = _dot(h.astype(BF16), w_ref[...])
    n_dn = dn_ref.shape[-1]
    n_sw = sw_ref.shape[-1]
    dn_ref[0] = p[:, :n_dn]
    sw_ref[0] = p[:, n_dn:n_dn + n_sw]
    hg_ref[0] = p[:, n_dn + n_sw:]


def _in_proj_call(x, mods_l, g1, w_in_p, ctx_len, tm):
    b, s, d = x.shape
    n_dn, n_sw, n_hg = 1152, 768, 1280
    n_all = w_in_p.shape[1]
    kern = functools.partial(_in_proj_kernel, ctx_tiles=ctx_len // tm, n_batch=b)
    return pl.pallas_call(
        kern,
        out_shape=(jax.ShapeDtypeStruct((b, s, n_dn), F32),
                   jax.ShapeDtypeStruct((b, s, n_sw), F32),
                   jax.ShapeDtypeStruct((b, s, n_hg), F32)),
        grid=(b, s // tm),
        in_specs=[
            pl.BlockSpec((1, tm, d), lambda i, t: (i, t, 0)),
            pl.BlockSpec(mods_l.shape, lambda i, t: (0, 0)),
            pl.BlockSpec((1, d), lambda i, t: (0, 0)),
            pl.BlockSpec((d, n_all), lambda i, t: (0, 0), pipeline_mode=pl.Buffered(1)),
        ],
        out_specs=(pl.BlockSpec((1, tm, n_dn), lambda i, t: (i, t, 0)),
                   pl.BlockSpec((1, tm, n_sw), lambda i, t: (i, t, 0)),
                   pl.BlockSpec((1, tm, n_hg), lambda i, t: (i, t, 0))),
        compiler_params=pltpu.CompilerParams(
            dimension_semantics=("arbitrary", "arbitrary"), vmem_limit_bytes=VMEM_LIMIT),
        name="in_proj",
    )(x, mods_l, g1.reshape(1, d), w_in_p)


def _out_mlp_kernel(x_ref, dn_ref, sw_ref, hg_ref, mods_ref, g2_ref, gf_ref, wo_ref, w1_ref, w2_ref, o_ref,
                    *, ctx_tiles, tile_offset, n_batch, final_norm):
    d = x_ref.shape[-1]
    t = pl.program_id(1) + tile_offset
    row = jnp.where(t < ctx_tiles, n_batch, pl.program_id(0))
    y = jnp.concatenate([dn_ref[0], sw_ref[0], hg_ref[0]], axis=-1)
    x1 = x_ref[0] + _mod_row(mods_ref, row, 2, d) * _dot(y, wo_ref[...])
    h = _rms(x1) * g2_ref[...] * (1.0 + _mod_row(mods_ref, row, 4, d)) + _mod_row(mods_ref, row, 3, d)
    hb = h.astype(BF16)
    d_ff = w1_ref.shape[1]
    ff_chunk = 1024
    acc = jnp.zeros_like(x1)
    for j in range(d_ff // ff_chunk):
        a = _dot(hb, w1_ref[:, j * ff_chunk:(j + 1) * ff_chunk])
        a = jnp.square(jnp.maximum(a, 0.0))
        acc = acc + _dot(a.astype(BF16), w2_ref[j * ff_chunk:(j + 1) * ff_chunk, :])
    x2 = x1 + _mod_row(mods_ref, row, 5, d) * acc
    if final_norm:
        x2 = _rms(x2) * gf_ref[...]
    o_ref[0] = x2


def _out_mlp_call(x, y_dn, y_sw, y_hg, mods_l, g2, gf, w_out, w1, w2, ctx_len, tm, latent_only):
    b, s, d = x.shape
    d_ff = w1.shape[1]
    off = ctx_len // tm if latent_only else 0
    s_out = s - ctx_len if latent_only else s
    kern = functools.partial(_out_mlp_kernel, ctx_tiles=ctx_len // tm, tile_offset=off, n_batch=b,
                             final_norm=latent_only)
    tok = lambda w: pl.BlockSpec((1, tm, w), lambda i, t: (i, t + off, 0))
    const = lambda shape: pl.BlockSpec(shape, lambda i, t: (0, 0))
    weight = lambda shape: pl.BlockSpec(shape, lambda i, t: (0, 0), pipeline_mode=pl.Buffered(1))
    return pl.pallas_call(
        kern,
        out_shape=jax.ShapeDtypeStruct((b, s_out, d), F32),
        grid=(b, s_out // tm),
        in_specs=[tok(d), tok(y_dn.shape[-1]), tok(y_sw.shape[-1]), tok(y_hg.shape[-1]),
                  const(mods_l.shape), const((1, d)), const((1, d)),
                  weight((d, d)), weight((d, d_ff)), weight((d_ff, d))],
        out_specs=pl.BlockSpec((1, tm, d), lambda i, t: (i, t, 0)),
        compiler_params=pltpu.CompilerParams(
            dimension_semantics=("arbitrary", "arbitrary"), vmem_limit_bytes=VMEM_LIMIT),
        name="out_mlp",
    )(x, y_dn, y_sw, y_hg, mods_l, g2.reshape(1, d), gf.reshape(1, d), w_out, w1, w2)


def _dn_kernel(p_ref, cw_ref, alog_ref, dtb_ref, ng_ref, o_ref, u_s, w_s, sc_s, qg_s, kd_s, eg_s, os_s, st_s,
               *, ctx_len):
    s_len = p_ref.shape[1]
    n_chunks = s_len // CHUNK
    n_ctx_chunks = ctx_len // CHUNK
    width = GROUP_LANES
    tile = 4 * CHUNK
    bdmask = _head_block_mask(width)
    bdones = _mask_bf16(bdmask)
    low, upp = _chunk_cumsum_mats()
    expand = _mask_bf16((_iota((256, 1024), 1) >> 6) == (_iota((256, 1024), 0) & 127))
    row = _iota((CHUNK, width), 0)
    col = _iota((CHUNK, width), 1) & 63
    eye = jnp.where(row == col, 1.0, 0.0)

    def solve_chunks(chains):
        n = len(chains)
        a_mats, scs, rhss, tails = [], [], [], []
        for d, q, k, v, bt, gcb in chains:
            if d == 0:
                incl, strict = col <= row, col < row
                gtot = gcb[CHUNK - 1:CHUNK, :]
            else:
                incl, strict = col >= row, col > row
                gtot = gcb[0:1, :]
            grow = jnp.sum(gcb * eye, axis=0, keepdims=True)
            decay = jnp.exp(jnp.where(incl, gcb - grow, NEG_INF))
            kb = k * bt
            egc = jnp.exp(gcb)
            kq = jnp.concatenate([kb, q], axis=0).astype(BF16)
            raw = _dot_nt(kq, _blockdiag(k, bdones))
            a_mats.append(jnp.where(strict, raw[0:CHUNK] * decay, 0.0))
            scs.append((raw[CHUNK:2 * CHUNK] * decay).astype(BF16))
            rhss.append(jnp.concatenate([_blockdiag(v * bt, bdones), _blockdiag(kb * egc, bdones)], axis=1))
            tails.append(((q * egc).astype(BF16), (k * jnp.exp(gtot - gcb)).astype(BF16), jnp.exp(gtot)))
        ps = [eye - jnp.where((row >> 1) == (col >> 1), a, 0.0) for a in a_mats]
        for lg in range(1, 6):
            pair = ((row >> (lg + 1)) == (col >> (lg + 1))) & ((row >> lg) != (col >> lg))
            cts = [_dot(jnp.where(pair, a_mats[i], 0.0).astype(BF16), _blockdiag(ps[i], bdones)) for i in range(n)]
            ps = [ps[i] - _dot(ps[i].astype(BF16), _blockdiag(cts[i], bdones)) for i in range(n)]
        outs = []
        for i in range(n):
            uw = _dot(ps[i].astype(BF16), rhss[i])
            outs.append((uw[:, 0:width], uw[:, width:2 * width].astype(BF16), scs[i]) + tails[i])
        return outs

    def prep_tile(t, carry):
        r0 = pl.multiple_of(t * tile, tile)
        top0 = pl.multiple_of(jnp.maximum(r0 - 8, 0), 8)
        bot0 = pl.multiple_of(jnp.minimum(r0 + tile, s_len - 8), 8)
        top = p_ref[0, pl.ds(top0, 8), 0:3 * width]
        mid = p_ref[0, pl.ds(r0, tile), 0:3 * width]
        bot = p_ref[0, pl.ds(bot0, 8), 0:3 * width]
        top_ok = jnp.logical_and(r0 != 0, r0 != ctx_len)
        bot_ok = jnp.logical_and(r0 + tile != ctx_len, r0 + tile != s_len)
        top = jnp.where(top_ok, top, 0.0)
        bot = jnp.where(bot_ok, bot, 0.0)
        xp = jnp.concatenate([top, mid, bot], axis=0)
        acc = xp[6:6 + tile, :] * cw_ref[0:1, :]
        for j in range(1, CONV_W):
            acc = acc + xp[6 + j:6 + j + tile, :] * cw_ref[j:j + 1, :]
        y = acc * _sigmoid(acc)
        q = y[:, 0:width]
        k = y[:, width:2 * width]
        v = y[:, 2 * width:3 * width]
        q = q * lax.rsqrt(_dot((q * q).astype(BF16), bdones) + EPS) * (HEAD_DIM ** -0.5)
        k = k * lax.rsqrt(_dot((k * k).astype(BF16), bdones) + EPS)
        ab = p_ref[0, pl.ds(r0, tile), 4 * width:4 * width + 128]
        hi, lo = _split(ab)
        abx = _dot(jnp.concatenate([hi, lo], axis=1), expand)
        g = -jnp.exp(alog_ref[...]) * _softplus(abx[:, 0:2 * width] + dtb_ref[...])
        beta = _sigmoid(abx[:, 2 * width:4 * width])
        cums = _chunk_cumsums(g, low, upp)
        os_s[pl.ds(r0, tile), :] = jnp.zeros((tile, width), F32)
        chains = []
        for cc in range(tile // CHUNK):
            rs = slice(cc * CHUNK, (cc + 1) * CHUNK)
            for d in range(2):
                chains.append((d, q[rs], k[rs], v[rs], beta[rs, d * width:(d + 1) * width], cums[d][rs]))
        for idx, (u, w, sc, qg, kd, eg) in enumerate(solve_chunks(chains)):
            cc, d = idx // 2, idx % 2
            rr = pl.ds(r0 + cc * CHUNK, CHUNK)
            er = pl.ds(pl.multiple_of((t * (tile // CHUNK) + cc) * 8, 8), 8)
            u_s[d, rr, :] = u
            w_s[d, rr, :] = w
            sc_s[d, rr, :] = sc
            qg_s[d, rr, :] = qg
            kd_s[d, rr, :] = kd
            eg_s[d, er, :] = jnp.broadcast_to(eg, (8, width))
        return carry

    lax.fori_loop(0, s_len // tile, prep_tile, 0)

    st_s[...] = jnp.zeros(st_s.shape, F32)

    def scan_step(i, carry):
        chunks = (i, _backward_chunk(i, n_ctx_chunks, n_chunks))
        rrs = [pl.ds(pl.multiple_of(c * CHUNK, CHUNK), CHUNK) for c in chunks]
        states = [st_s[d] for d in range(2)]
        ws_qs = [_dot(jnp.concatenate([w_s[d, rrs[d], :], qg_s[d, rrs[d], :]], axis=0), states[d].astype(BF16))
                 for d in range(2)]
        v_new = [u_s[d, rrs[d], :] - ws_qs[d][0:CHUNK] for d in range(2)]
        upd = [_dot_tn(kd_s[d, rrs[d], :], v_new[d].astype(BF16)) for d in range(2)]
        for d in range(2):
            eg = eg_s[d, pl.ds(pl.multiple_of(chunks[d] * 8, 8), 1), :]
            st_s[d] = states[d] * eg + jnp.where(bdmask, upd[d], 0.0)
        outs = [ws_qs[d][CHUNK:2 * CHUNK] + _dot(sc_s[d, rrs[d], :], _blockdiag(v_new[d], bdones)) for d in range(2)]
        for d in range(2):
            os_s[rrs[d], :] = os_s[rrs[d], :] + outs[d]
        return carry

    lax.fori_loop(0, n_chunks, scan_step, 0)

    def out_tile(t, carry):
        r0 = pl.multiple_of(t * 256, 256)
        o = os_s[pl.ds(r0, 256), :]
        ms = _dot((o * o).astype(BF16), bdones) * (1.0 / HEAD_DIM)
        gate = p_ref[0, pl.ds(r0, 256), 3 * width:4 * width]
        y = o * lax.rsqrt(ms + EPS) * ng_ref[...] * (gate * _sigmoid(gate))
        o_ref[0, pl.ds(r0, 256), :] = y.astype(o_ref.dtype)
        return carry

    lax.fori_loop(0, s_len // 256, out_tile, 0)


def _dn_call(p_dn, conv_w, a_log, dt_bias, norm_g, ctx_len):
    b, s, n = p_dn.shape
    width = GROUP_LANES
    lanes = lambda a: jnp.repeat(a.astype(F32).reshape(-1), HEAD_DIM).reshape(1, 2 * width)
    ng = jnp.tile(norm_g.astype(F32), width // HEAD_DIM).reshape(1, width)
    const = lambda shape: pl.BlockSpec(shape, lambda i: (0, 0))
    return pl.pallas_call(
        functools.partial(_dn_kernel, ctx_len=ctx_len),
        out_shape=jax.ShapeDtypeStruct((b, s, width), BF16),
        grid=(b,),
        in_specs=[pl.BlockSpec((1, s, n), lambda i: (i, 0, 0)),
                  const((CONV_W, 3 * width)), const((1, 2 * width)), const((1, 2 * width)), const((1, width))],
        out_specs=pl.BlockSpec((1, s, width), lambda i: (i, 0, 0)),
        scratch_shapes=[pltpu.VMEM((2, s, width), F32)] + [pltpu.VMEM((2, s, width), BF16)] * 4 + [
            pltpu.VMEM((2, s // CHUNK * 8, width), F32), pltpu.VMEM((s, width), F32),
            pltpu.VMEM((2, width, width), F32)],
        compiler_params=pltpu.CompilerParams(dimension_semantics=("arbitrary",), vmem_limit_bytes=VMEM_LIMIT),
        name="deltanet",
    )(p_dn, conv_w.astype(F32), lanes(a_log), lanes(dt_bias), ng)


def _hg_kernel(p_ref, lb_ref, ng_ref, o_ref, bc_s, kk_s, os_s, st_s, *, ctx_len):
    s_len = p_ref.shape[1]
    n_chunks = s_len // CHUNK
    n_ctx_chunks = ctx_len // CHUNK
    width = GROUP_LANES
    n_sub = CHUNK // HG_SUB
    bdmask = _head_block_mask(width)
    bdones = _mask_bf16(bdmask)
    low, upp = _chunk_cumsum_mats()
    lb = jnp.concatenate([lb_ref[0:1, :], lb_ref[1:2, :]], axis=1)
    log_lb = jnp.log(lb)
    log_1m = jnp.log1p(-lb)

    def gate_tile(t, carry):
        r0 = pl.multiple_of(t * 256, 256)
        z = p_ref[0, pl.ds(r0, 256), width:3 * width]
        ez = jnp.exp(-jnp.abs(z))
        l1p = jnp.log1p(ez)
        tt = log_1m + (jnp.minimum(z, 0.0) - l1p)
        logf = jnp.maximum(log_lb, tt) + jnp.log1p(jnp.exp(-jnp.abs(log_lb - tt)))
        kk_s[pl.ds(r0, 256), :] = (1.0 - lb) * _sigmoid(-z)
        c0, c1 = _chunk_cumsums(logf, low, upp)
        bc_s[pl.ds(r0, 256), 0:width] = c0
        bc_s[pl.ds(r0, 256), width:2 * width] = c1
        os_s[pl.ds(r0, 256), :] = jnp.zeros((256, width), F32)
        return carry

    lax.fori_loop(0, s_len // 256, gate_tile, 0)

    st_s[...] = jnp.zeros(st_s.shape, F32)
    row = _iota((CHUNK, width), 0)
    col = _iota((CHUNK, width), 1) & 63
    row_sub = row >> 3
    col_sub = col >> 3

    def chunk_dir(d, c):
        r0 = pl.multiple_of(c * CHUNK, CHUNK)
        q = p_ref[0, pl.ds(r0, CHUNK), 0:width]
        v = p_ref[0, pl.ds(r0, CHUNK), 3 * width:4 * width]
        k = kk_s[pl.ds(r0, CHUNK), d * width:(d + 1) * width]
        bcb = bc_s[pl.ds(r0, CHUNK), d * width:(d + 1) * width]
        if d == 0:
            edge = lambda j: bcb[HG_SUB * j + HG_SUB - 1:HG_SUB * j + HG_SUB, :]
            subs = range(0, n_sub - 1)
            b_last = bcb[CHUNK - 1:CHUNK, :]
        else:
            edge = lambda j: bcb[HG_SUB * j:HG_SUB * j + 1, :]
            subs = range(1, n_sub)
            b_last = bcb[0:1, :]
        key_edge = jnp.concatenate([jnp.broadcast_to(edge(j), (HG_SUB, width)) for j in range(n_sub)], axis=0)
        k_til = k * jnp.exp(jnp.minimum(key_edge - bcb, 0.0))
        kbd = _blockdiag(k_til, bdones)
        q_stack = jnp.concatenate(
            [(q * jnp.exp(jnp.minimum(bcb - edge(j), 0.0))).astype(BF16) for j in subs], axis=0)
        off = _dot_nt(q_stack, kbd)
        scores = jnp.zeros((CHUNK, width), F32)
        for n, j in enumerate(subs):
            sel = (col_sub == j) & ((row_sub > j) if d == 0 else (row_sub < j))
            scores = jnp.where(sel, off[n * CHUNK:(n + 1) * CHUNK], scores)
        terms = []
        for delta in range(HG_SUB):
            shift = delta if d == 0 else (CHUNK - delta) % CHUNK
            if delta == 0:
                terms.append((q * k).astype(BF16))
            else:
                ks = pltpu.roll(k, shift, 0)
                bs = pltpu.roll(bcb, shift, 0)
                terms.append((q * ks * jnp.exp(jnp.minimum(bcb - bs, 0.0))).astype(BF16))
        red = _dot(jnp.concatenate(terms, axis=0), bdones)
        for delta in range(HG_SUB):
            if d == 0:
                sel = (col == row - delta) & ((row & (HG_SUB - 1)) >= delta)
            else:
                sel = (col == row + delta) & ((row & (HG_SUB - 1)) < HG_SUB - delta)
            scores = jnp.where(sel, red[delta * CHUNK:(delta + 1) * CHUNK], scores)
        state_t = st_s[d]
        o = _dot(scores.astype(BF16), _blockdiag(v, bdones)) + _dot_nt(
            (q * jnp.exp(bcb)).astype(BF16), state_t.astype(BF16))
        k_dec = (k * jnp.exp(b_last - bcb)).astype(BF16)
        upd = _dot_tn(v.astype(BF16), k_dec)
        st_s[d] = state_t * jnp.exp(b_last) + jnp.where(bdmask, upd, 0.0)
        os_s[pl.ds(r0, CHUNK), :] = os_s[pl.ds(r0, CHUNK), :] + o

    def scan_step(i, carry):
        chunk_dir(0, i)
        chunk_dir(1, _backward_chunk(i, n_ctx_chunks, n_chunks))
        return carry

    lax.fori_loop(0, n_chunks, scan_step, 0)

    def out_tile(t, carry):
        r0 = pl.multiple_of(t * 256, 256)
        o = os_s[pl.ds(r0, 256), :]
        ms = _dot((o * o).astype(BF16), bdones) * (1.0 / HEAD_DIM)
        gate = p_ref[0, pl.ds(r0, 256), 4 * width:5 * width]
        y = o * lax.rsqrt(ms + EPS) * ng_ref[...] * (gate * _sigmoid(gate))
        o_ref[0, pl.ds(r0, 256), :] = y.astype(o_ref.dtype)
        return carry

    lax.fori_loop(0, s_len // 256, out_tile, 0)


def _hg_call(p_hg, lb_l, norm_g, ctx_len):
    b, s, n = p_hg.shape
    width = GROUP_LANES
    ng = jnp.tile(norm_g.astype(F32), width // HEAD_DIM).reshape(1, width)
    const = lambda shape: pl.BlockSpec(shape, lambda i: (0, 0))
    return pl.pallas_call(
        functools.partial(_hg_kernel, ctx_len=ctx_len),
        out_shape=jax.ShapeDtypeStruct((b, s, width), BF16),
        grid=(b,),
        in_specs=[pl.BlockSpec((1, s, n), lambda i: (i, 0, 0)), const((2, width)), const((1, width))],
        out_specs=pl.BlockSpec((1, s, width), lambda i: (i, 0, 0)),
        scratch_shapes=[pltpu.VMEM((s, 2 * width), F32), pltpu.VMEM((s, 2 * width), F32),
                        pltpu.VMEM((s, width), F32), pltpu.VMEM((2, width, width), F32)],
        compiler_params=pltpu.CompilerParams(dimension_semantics=("arbitrary",), vmem_limit_bytes=VMEM_LIMIT),
        name="hgrn2",
    )(p_hg, lb_l, ng)


def _swa_kernel(sink_ref, p_ref, cos_ref, sin_ref, o_ref, q_s, k_s, v_s, *, ctx_len):
    s_len = p_ref.shape[1]
    t_len = s_len - ctx_len
    blk = SWA_BLOCK
    qw = SWA_HEADS * HEAD_DIM
    kvw = SWA_KV_HEADS * HEAD_DIM
    group = SWA_HEADS // SWA_KV_HEADS
    gw = group * HEAD_DIM
    scale = HEAD_DIM ** -0.5
    half = HEAD_DIM // 2
    lane128 = _iota((blk, kvw), 1)
    first_half_q = (_iota((blk, qw), 1) & (HEAD_DIM - 1)) < half
    first_half_k = (lane128 & (HEAD_DIM - 1)) < half
    low_head = lane128 < HEAD_DIM

    def kv_tiles(x):
        sw = pltpu.roll(x, HEAD_DIM, 1)
        return jnp.where(low_head, x, sw), jnp.where(low_head, sw, x)

    def prep_tile(t, carry):
        r0 = pl.multiple_of(t * blk, blk)
        q = p_ref[0, pl.ds(r0, blk), 0:qw]
        k = p_ref[0, pl.ds(r0, blk), qw:qw + kvw]
        v = p_ref[0, pl.ds(r0, blk), qw + kvw:qw + 2 * kvw]
        pos = pl.multiple_of(jnp.maximum(r0 - ctx_len, 0), blk)
        cs = cos_ref[pl.ds(pos, blk), :]
        sn = sin_ref[pl.ds(pos, blk), :]
        is_ctx = r0 < ctx_len
        cs = jnp.where(is_ctx, 1.0, cs)
        sn = jnp.where(is_ctx, 0.0, sn)
        q_sw = jnp.where(first_half_q, pltpu.roll(q, qw - half, 1), pltpu.roll(q, half, 1))
        k_sw = jnp.where(first_half_k, pltpu.roll(k, kvw - half, 1), pltpu.roll(k, half, 1))
        cs4 = jnp.concatenate([cs] * (qw // kvw), axis=1)
        sn4 = jnp.concatenate([sn] * (qw // kvw), axis=1)
        q_s[pl.ds(r0, blk), :] = ((q * cs4 + q_sw * sn4) * scale).astype(BF16)
        k0, k1 = kv_tiles(k * cs + k_sw * sn)
        v0, v1 = kv_tiles(v)
        k_s[0, pl.ds(r0, blk), :] = k0.astype(BF16)
        k_s[1, pl.ds(r0, blk), :] = k1.astype(BF16)
        v_s[0, pl.ds(r0, blk), :] = v0.astype(BF16)
        v_s[1, pl.ds(r0, blk), :] = v1.astype(BF16)
        return carry

    lax.fori_loop(0, s_len // blk, prep_tile, 0)

    rows = group * blk
    blk_shift = blk.bit_length() - 1
    qmask = _mask_bf16((_iota((rows, gw), 0) >> blk_shift) == (_iota((rows, gw), 1) >> 6))
    rb1 = _iota((rows, 1), 0) >> blk_shift
    r_in = _iota((rows, 3 * blk), 0) & (blk - 1)
    c_in = _iota((rows, 3 * blk), 1)
    lane_g = _iota((blk, gw), 1) >> 6

    def wide(x):
        return jnp.concatenate([x, x], axis=1)

    def attend(r0, g, local_start):
        qb = q_s[pl.ds(r0, blk), g * gw:(g + 1) * gw]
        qst = jnp.concatenate([qb] * group, axis=0) * qmask
        sink = jnp.zeros((rows, 1), F32)
        for j in range(group):
            sink = jnp.where(rb1 == j, sink_ref[g * group + j], sink)
        kc = wide(k_s[g, 0:ctx_len, :])
        vc = wide(v_s[g, 0:ctx_len, :])
        s_ctx = _dot_nt(qst, kc)
        m = jnp.maximum(jnp.max(s_ctx, axis=-1, keepdims=True), sink)
        if local_start is not None:
            kl = wide(k_s[g, pl.ds(ctx_len + local_start, 3 * blk), :])
            vl = wide(v_s[g, pl.ds(ctx_len + local_start, 3 * blk), :])
            qpos = (r0 - ctx_len) + r_in
            kpos = local_start + c_in
            valid = jnp.abs(qpos - kpos) <= WINDOW
            s_loc = jnp.where(valid, _dot_nt(qst, kl), NEG_INF)
            m = jnp.maximum(m, jnp.max(s_loc, axis=-1, keepdims=True))
            e_loc = jnp.exp(s_loc - m)
        e_ctx = jnp.exp(s_ctx - m)
        den = jnp.sum(e_ctx, axis=-1, keepdims=True) + jnp.exp(sink - m)
        acc = _dot(e_ctx.astype(BF16), vc)
        if local_start is not None:
            den = den + jnp.sum(e_loc, axis=-1, keepdims=True)
            acc = acc + _dot(e_loc.astype(BF16), vl)
        acc = acc / den
        out = jnp.zeros((blk, gw), F32)
        for j in range(group):
            out = jnp.where(lane_g == j, acc[j * blk:(j + 1) * blk], out)
        o_ref[0, pl.ds(r0, blk), g * gw:(g + 1) * gw] = out.astype(o_ref.dtype)

    def ctx_block(t, carry):
        r0 = pl.multiple_of(t * blk, blk)
        for g in range(SWA_KV_HEADS):
            attend(r0, g, None)
        return carry

    lax.fori_loop(0, ctx_len // blk, ctx_block, 0)

    def lat_block(t, carry):
        r0 = pl.multiple_of(ctx_len + t * blk, blk)
        start = pl.multiple_of(jnp.clip((t - 1) * blk, 0, t_len - 3 * blk), blk)
        for g in range(SWA_KV_HEADS):
            attend(r0, g, start)
        return carry

    lax.fori_loop(0, t_len // blk, lat_block, 0)


def _swa_call(p_sw, sink, cos_t, sin_t, ctx_len):
    b, s, n = p_sw.shape
    qw = SWA_HEADS * HEAD_DIM
    kvw = SWA_KV_HEADS * HEAD_DIM
    t_len = s - ctx_len
    return pl.pallas_call(
        functools.partial(_swa_kernel, ctx_len=ctx_len),
        out_shape=jax.ShapeDtypeStruct((b, s, qw), BF16),
        grid=(b,),
        in_specs=[pl.BlockSpec(memory_space=pltpu.SMEM),
                  pl.BlockSpec((1, s, n), lambda i: (i, 0, 0)),
                  pl.BlockSpec((t_len, kvw), lambda i: (0, 0)),
                  pl.BlockSpec((t_len, kvw), lambda i: (0, 0))],
        out_specs=pl.BlockSpec((1, s, qw), lambda i: (i, 0, 0)),
        scratch_shapes=[pltpu.VMEM((s, qw), BF16), pltpu.VMEM((SWA_KV_HEADS, s, kvw), BF16),
                        pltpu.VMEM((SWA_KV_HEADS, s, kvw), BF16)],
        compiler_params=pltpu.CompilerParams(dimension_semantics=("arbitrary",), vmem_limit_bytes=VMEM_LIMIT),
        name="swa",
    )(sink.astype(F32), p_sw, cos_t, sin_t)


def _rope_tables(t_len):
    rows = t_len // GRID_W
    row = jnp.repeat(jnp.arange(rows), GRID_W).astype(F32)
    col = jnp.tile(jnp.arange(GRID_W), rows).astype(F32)
    half = HEAD_DIM // 2
    inv = ROPE_BASE ** (-jnp.arange(0, half, 2, dtype=F32) / half)
    ang = jnp.concatenate([row[:, None] * inv, col[:, None] * inv], axis=-1)
    cos, sin = jnp.cos(ang), jnp.sin(ang)
    return jnp.tile(cos, (1, 4)), jnp.tile(jnp.concatenate([-sin, sin], axis=-1), (1, 2))


def kernel(x, c, ctx, c_ctx, w_ada, b_ada, norm1, norm2, w_in, dn_conv, dn_A_log, dn_dt_bias, dn_norm, swa_sink,
           hg_lb_logits, hg_norm, w_out, w_ff1, w_ff2, norm_f):
    b, t_len, d = x.shape
    ctx_len = ctx.shape[1]
    depth = w_ada.shape[0]
    tm = 256
    assert b + 1 <= MOD_ROWS and ctx_len % tm == 0 and t_len % tm == 0 and t_len >= 3 * SWA_BLOCK

    cc = jnp.zeros((MOD_ROWS, d), F32).at[:b].set(c.astype(F32)).at[b].set(c_ctx.astype(F32))
    mods = _ada_call(cc, w_ada.astype(F32), b_ada.astype(F32))
    lb_all = _lb_call(hg_lb_logits)
    cos_t, sin_t = _rope_tables(t_len)

    n_head = 4 * GROUP_LANES + 16
    w_in_p = jnp.concatenate(
        [w_in[:, :, :n_head], jnp.zeros((depth, d, 128 - 16), w_in.dtype), w_in[:, :, n_head:]], axis=-1).astype(BF16)
    w_out_b = w_out.astype(BF16)
    w1_b = w_ff1.astype(BF16)
    w2_b = w_ff2.astype(BF16)

    xs = jnp.concatenate([ctx.astype(F32), x.astype(F32)], axis=1)
    for l in range(depth):
        last = l == depth - 1
        p_dn, p_sw, p_hg = _in_proj_call(xs, mods[l], norm1[l].astype(F32), w_in_p[l], ctx_len, tm)
        y_dn = _dn_call(p_dn, dn_conv[l], dn_A_log[l], dn_dt_bias[l], dn_norm[l], ctx_len)
        y_sw = _swa_call(p_sw, swa_sink[l], cos_t, sin_t, ctx_len)
        y_hg = _hg_call(p_hg, lb_all[:, l], hg_norm[l], ctx_len)
        xs = _out_mlp_call(xs, y_dn, y_sw, y_hg, mods[l], norm2[l].astype(F32), norm_f.astype(F32),
                           w_out_b[l], w1_b[l], w2_b[l], ctx_len, tm, latent_only=last)
    return xs.astype(x.dtype)
```

```python
import functools
import math

import jax
import jax.numpy as jnp
from jax import lax
from jax.experimental import pallas as pl
from jax.experimental.pallas import tpu as pltpu

F32 = jnp.float32
BF16 = jnp.bfloat16

HEAD_DIM = 64
GROUP_LANES = 256
CHUNK = 64
N_MOD = 6
EPS = 1e-6
CONV_W = 5
SWA_HEADS = 8
SWA_KV_HEADS = 2
SWA_BLOCK = 128
WINDOW = 128
GRID_W = 64
ROPE_BASE = 10000.0
MOD_ROWS = 24
VMEM_LIMIT = 56 * 1024 * 1024
NEG_INF = float("-inf")


def _dot(a, b):
    return jnp.dot(a, b, preferred_element_type=F32)


def _dot_nt(a, b):
    return lax.dot_general(a, b, (((1,), (1,)), ((), ())), preferred_element_type=F32)


def _dot_tn(a, b):
    return lax.dot_general(a, b, (((0,), (0,)), ((), ())), preferred_element_type=F32)


def _split(x):
    hi = x.astype(BF16)
    lo = (x - hi.astype(F32)).astype(BF16)
    return hi, lo


def _iota(shape, dim):
    return lax.broadcasted_iota(jnp.int32, shape, dim)


def _sigmoid(x):
    return 1.0 / (1.0 + jnp.exp(-x))


def _softplus(x):
    return jnp.maximum(x, 0.0) + jnp.log1p(jnp.exp(-jnp.abs(x)))


def _head_block_mask(n):
    return (_iota((n, n), 0) >> 6) == (_iota((n, n), 1) >> 6)


def _mask_bf16(cond):
    return jnp.where(cond, 1.0, 0.0).astype(BF16)


def _blockdiag(x, ones_bd):
    xb = x.astype(BF16)
    return jnp.concatenate([xb, xb, xb, xb], axis=0) * ones_bd


def _chunk_cumsum_mats():
    r = _iota((256, 512), 0)
    c = _iota((256, 512), 1) & 255
    same = (r >> 6) == (c >> 6)
    return _mask_bf16(same & (c <= r)), _mask_bf16(same & (c >= r))


def _chunk_cumsums(x, low, upp):
    hi, lo = _split(x)
    c0 = _dot(low, jnp.concatenate([hi[:, :256], lo[:, :256]], axis=0))
    c1 = _dot(upp, jnp.concatenate([hi[:, 256:], lo[:, 256:]], axis=0))
    return c0, c1


def _backward_chunk(i, n_ctx_chunks, n_chunks):
    return jnp.where(i < n_ctx_chunks, n_ctx_chunks - 1 - i, n_chunks - 1 + n_ctx_chunks - i)


def _ada_kernel(c_ref, w_ref, b_ref, o_ref):
    c = c_ref[...]
    a = c * _sigmoid(c)
    a_hi, a_lo = _split(a)
    w = w_ref[0]
    w_hi, w_lo = _split(w)
    acc = _dot(a_hi, w_hi) + _dot(a_lo, w_hi) + _dot(a_hi, w_lo)
    o_ref[0] = acc + b_ref[0]


def _ada_call(cc, w_ada, b_ada):
    depth, d, n = w_ada.shape
    tn = 1536
    return pl.pallas_call(
        _ada_kernel,
        out_shape=jax.ShapeDtypeStruct((depth, MOD_ROWS, n), F32),
        grid=(depth, n // tn),
        in_specs=[
            pl.BlockSpec((MOD_ROWS, d), lambda l, j: (0, 0)),
            pl.BlockSpec((1, d, tn), lambda l, j: (l, 0, j)),
            pl.BlockSpec((1, 1, tn), lambda l, j: (l, 0, j)),
        ],
        out_specs=pl.BlockSpec((1, MOD_ROWS, tn), lambda l, j: (l, 0, j)),
        compiler_params=pltpu.CompilerParams(
            dimension_semantics=("arbitrary", "arbitrary"), vmem_limit_bytes=VMEM_LIMIT),
        name="ada_mod",
    )(cc, w_ada, b_ada.reshape(depth, 1, n))


def _lb_kernel(x_ref, o_ref):
    depth = x_ref.shape[1]
    rows = [x_ref[:, l, :] for l in range(depth)]
    m = rows[0]
    for r in rows[1:]:
        m = jnp.maximum(m, r)
    es = [jnp.exp(r - m) for r in rows]
    tot = es[0]
    for e in es[1:]:
        tot = tot + e
    ps = [e / tot for e in es]
    run = ps[0]
    o_ref[:, 0, :] = run - ps[0]
    for l in range(1, depth):
        run = run + ps[l]
        o_ref[:, l, :] = run - ps[0]


def _lb_call(logits):
    return pl.pallas_call(
        _lb_kernel,
        out_shape=jax.ShapeDtypeStruct(logits.shape, F32),
        name="hg_lower_bounds",
    )(logits.astype(F32))


def _mod_row(mods_ref, row, idx, d):
    return mods_ref[pl.ds(row, 1), idx * d:(idx + 1) * d]


def _rms(x):
    return x * lax.rsqrt(jnp.mean(x * x, axis=-1, keepdims=True) + EPS)


def _in_proj_kernel(x_ref, mods_ref, g_ref, w_ref, dn_ref, sw_ref, hg_ref, *, ctx_tiles, n_batch):
    d = x_ref.shape[-1]
    row = jnp.where(pl.program_id(1) < ctx_tiles, n_batch, pl.program_id(0))
    shift = _mod_row(mods_ref, row, 0, d)
    scale = _mod_row(mods_ref, row, 1, d)
    h = _rms(x_ref[0]) * g_ref[...] * (1.0 + scale) + shift
    p = _dot(h.astype(BF16), w_ref[...])
    n_dn = dn_ref.shape[-1]
    n_sw = sw_ref.shape[-1]
    dn_ref[0] = p[:, :n_dn]
    sw_ref[0] = p[:, n_dn:n_dn + n_sw]
    hg_ref[0] = p[:, n_dn + n_sw:]


def _in_proj_call(x, mods_l, g1, w_in_p, ctx_len, tm):
    b, s, d = x.shape
    n_dn, n_sw, n_hg = 1152, 768, 1280
    n_all = w_in_p.shape[1]
    kern = functools.partial(_in_proj_kernel, ctx_tiles=ctx_len // tm, n_batch=b)
    return pl.pallas_call(
        kern,
        out_shape=(jax.ShapeDtypeStruct((b, s, n_dn), F32),
                   jax.ShapeDtypeStruct((b, s, n_sw), F32),
                   jax.ShapeDtypeStruct((b, s, n_hg), F32)),
        grid=(b, s // tm),
        in_specs=[
            pl.BlockSpec((1, tm, d), lambda i, t: (i, t, 0)),
            pl.BlockSpec(mods_l.shape, lambda i, t: (0, 0)),
            pl.BlockSpec((1, d), lambda i, t: (0, 0)),
            pl.BlockSpec((d, n_all), lambda i, t: (0, 0), pipeline_mode=pl.Buffered(1)),
        ],
        out_specs=(pl.BlockSpec((1, tm, n_dn), lambda i, t: (i, t, 0)),
                   pl.BlockSpec((1, tm, n_sw), lambda i, t: (i, t, 0)),
                   pl.BlockSpec((1, tm, n_hg), lambda i, t: (i, t, 0))),
        compiler_params=pltpu.CompilerParams(
            dimension_semantics=("arbitrary", "arbitrary"), vmem_limit_bytes=VMEM_LIMIT),
        name="in_proj",
    )(x, mods_l, g1.reshape(1, d), w_in_p)


def _out_mlp_kernel(x_ref, dn_ref, sw_ref, hg_ref, mods_ref, g2_ref, gf_ref, wo_ref, w1_ref, w2_ref, o_ref,
                    *, ctx_tiles, tile_offset, n_batch, final_norm):
    d = x_ref.shape[-1]
    t = pl.program_id(1) + tile_offset
    row = jnp.where(t < ctx_tiles, n_batch, pl.program_id(0))
    y = jnp.concatenate([dn_ref[0], sw_ref[0], hg_ref[0]], axis=-1)
    x1 = x_ref[0] + _mod_row(mods_ref, row, 2, d) * _dot(y, wo_ref[...])
    h = _rms(x1) * g2_ref[...] * (1.0 + _mod_row(mods_ref, row, 4, d)) + _mod_row(mods_ref, row, 3, d)
    hb = h.astype(BF16)
    d_ff = w1_ref.shape[1]
    ff_chunk = 1024
    acc = jnp.zeros_like(x1)
    for j in range(d_ff // ff_chunk):
        a = _dot(hb, w1_ref[:, j * ff_chunk:(j + 1) * ff_chunk])
        a = jnp.square(jnp.maximum(a, 0.0))
        acc = acc + _dot(a.astype(BF16), w2_ref[j * ff_chunk:(j + 1) * ff_chunk, :])
    x2 = x1 + _mod_row(mods_ref, row, 5, d) * acc
    if final_norm:
        x2 = _rms(x2) * gf_ref[...]
    o_ref[0] = x2


def _out_mlp_call(x, y_dn, y_sw, y_hg, mods_l, g2, gf, w_out, w1, w2, ctx_len, tm, latent_only):
    b, s, d = x.shape
    d_ff = w1.shape[1]
    off = ctx_len // tm if latent_only else 0
    s_out = s - ctx_len if latent_only else s
    kern = functools.partial(_out_mlp_kernel, ctx_tiles=ctx_len // tm, tile_offset=off, n_batch=b,
                             final_norm=latent_only)
    tok = lambda w: pl.BlockSpec((1, tm, w), lambda i, t: (i, t + off, 0))
    const = lambda shape: pl.BlockSpec(shape, lambda i, t: (0, 0))
    weight = lambda shape: pl.BlockSpec(shape, lambda i, t: (0, 0), pipeline_mode=pl.Buffered(1))
    return pl.pallas_call(
        kern,
        out_shape=jax.ShapeDtypeStruct((b, s_out, d), F32),
        grid=(b, s_out // tm),
        in_specs=[tok(d), tok(y_dn.shape[-1]), tok(y_sw.shape[-1]), tok(y_hg.shape[-1]),
                  const(mods_l.shape), const((1, d)), const((1, d)),
                  weight((d, d)), weight((d, d_ff)), weight((d_ff, d))],
        out_specs=pl.BlockSpec((1, tm, d), lambda i, t: (i, t, 0)),
        compiler_params=pltpu.CompilerParams(
            dimension_semantics=("arbitrary", "arbitrary"), vmem_limit_bytes=VMEM_LIMIT),
        name="out_mlp",
    )(x, y_dn, y_sw, y_hg, mods_l, g2.reshape(1, d), gf.reshape(1, d), w_out, w1, w2)


def _dn_kernel(p_ref, cw_ref, alog_ref, dtb_ref, ng_ref, o_ref, xs_s, u_s, w_s, sc_s, qg_s, kd_s, eg_s, os_s, st_s,
               *, ctx_len):
    s_len = p_ref.shape[1]
    n_chunks = s_len // CHUNK
    n_ctx_chunks = ctx_len // CHUNK
    width = GROUP_LANES
    sub = 4 * CHUNK
    tile = sub
    bdmask = _head_block_mask(width)
    bdones = _mask_bf16(bdmask)
    low, upp = _chunk_cumsum_mats()
    expand = _mask_bf16((_iota((256, 1024), 1) >> 6) == (_iota((256, 1024), 0) & 127))
    row = _iota((CHUNK, width), 0)
    col = _iota((CHUNK, width), 1) & 63
    eye = jnp.where(row == col, 1.0, 0.0)

    def solve_chunks(chains):
        n = len(chains)
        a_mats, scs, rhss, tails = [], [], [], []
        for d, q, k, v, bt, gcb in chains:
            if d == 0:
                incl, strict = col <= row, col < row
                gtot = gcb[CHUNK - 1:CHUNK, :]
            else:
                incl, strict = col >= row, col > row
                gtot = gcb[0:1, :]
            grow = jnp.sum(gcb * eye, axis=0, keepdims=True)
            decay = jnp.exp(jnp.where(incl, gcb - grow, NEG_INF))
            kb = k * bt
            egc = jnp.exp(gcb)
            kq = jnp.concatenate([kb, q], axis=0).astype(BF16)
            raw = _dot_nt(kq, _blockdiag(k, bdones))
            a_mats.append(jnp.where(strict, raw[0:CHUNK] * decay, 0.0))
            scs.append((raw[CHUNK:2 * CHUNK] * decay).astype(BF16))
            rhss.append(jnp.concatenate([_blockdiag(v * bt, bdones), _blockdiag(kb * egc, bdones)], axis=1))
            tails.append(((q * egc).astype(BF16), (k * jnp.exp(gtot - gcb)).astype(BF16), jnp.exp(gtot)))
        ps = [eye - jnp.where((row >> 1) == (col >> 1), a, 0.0) for a in a_mats]
        for lg in range(1, 6):
            pair = ((row >> (lg + 1)) == (col >> (lg + 1))) & ((row >> lg) != (col >> lg))
            cts = [_dot(jnp.where(pair, a_mats[i], 0.0).astype(BF16), _blockdiag(ps[i], bdones)) for i in range(n)]
            ps = [ps[i] - _dot(ps[i].astype(BF16), _blockdiag(cts[i], bdones)) for i in range(n)]
        outs = []
        for i in range(n):
            uw = _dot(ps[i].astype(BF16), rhss[i])
            outs.append((uw[:, 0:width], uw[:, width:2 * width].astype(BF16), scs[i]) + tails[i])
        return outs

    gap = 8
    n_slabs = 3 * width // 128
    zeros_gap = jnp.zeros((n_slabs, gap, 128), F32)
    xs_s[:, 0:gap, :] = zeros_gap
    xs_s[:, gap + ctx_len:2 * gap + ctx_len, :] = zeros_gap
    xs_s[:, 2 * gap + s_len:3 * gap + s_len, :] = zeros_gap

    def padded_row(r0):
        return r0 + gap + jnp.where(r0 >= ctx_len, gap, 0)

    def copy_tile(t, carry):
        r0 = pl.multiple_of(t * sub, sub)
        dst = pl.ds(pl.multiple_of(padded_row(r0), 8), sub)
        for c in range(n_slabs):
            xs_s[c, dst, :] = p_ref[0, pl.ds(r0, sub), c * 128:(c + 1) * 128]
        return carry

    lax.fori_loop(0, s_len // sub, copy_tile, 0)
    lane_ab = _iota((sub, 128), 1)

    def conv_tap(base, j):
        rows = pl.ds(base + j, sub)
        return jnp.concatenate([xs_s[c, rows, :] for c in range(n_slabs)], axis=1) * cw_ref[j:j + 1, :]

    def sub_tile_chains(r0):
        base = padded_row(r0) - CONV_W // 2
        acc = conv_tap(base, 0)
        for j in range(1, CONV_W):
            acc = acc + conv_tap(base, j)
        y = acc * _sigmoid(acc)
        q = y[:, 0:width]
        k = y[:, width:2 * width]
        v = y[:, 2 * width:3 * width]
        q = q * lax.rsqrt(_dot((q * q).astype(BF16), bdones) + EPS) * (HEAD_DIM ** -0.5)
        k = k * lax.rsqrt(_dot((k * k).astype(BF16), bdones) + EPS)
        ab = p_ref[0, pl.ds(r0, sub), 4 * width:4 * width + 128]
        narrow = jnp.where(lane_ab < 8, -jnp.exp(alog_ref[...]) * _softplus(ab + dtb_ref[...]), _sigmoid(ab))
        hi, lo = _split(narrow)
        abx = _dot(jnp.concatenate([hi, lo], axis=1), expand)
        cums = _chunk_cumsums(abx[:, 0:2 * width], low, upp)
        beta = abx[:, 2 * width:4 * width]
        chains = []
        for cc in range(sub // CHUNK):
            rs = slice(cc * CHUNK, (cc + 1) * CHUNK)
            for d in range(2):
                chains.append((d, q[rs], k[rs], v[rs], beta[rs, d * width:(d + 1) * width], cums[d][rs]))
        return chains

    def prep_tile(t, carry):
        r0 = pl.multiple_of(t * tile, tile)
        os_s[pl.ds(r0, tile), :] = jnp.zeros((tile, width), F32)
        chains = []
        for sb in range(tile // sub):
            chains += sub_tile_chains(pl.multiple_of(r0 + sb * sub, sub))
        for idx, (u, w, sc, qg, kd, eg) in enumerate(solve_chunks(chains)):
            cc, d = idx // 2, idx % 2
            rr = pl.ds(r0 + cc * CHUNK, CHUNK)
            er = pl.ds(pl.multiple_of((t * (tile // CHUNK) + cc) * 8, 8), 8)
            u_s[d, rr, :] = u
            w_s[d, rr, :] = w
            sc_s[d, rr, :] = sc
            qg_s[d, rr, :] = qg
            kd_s[d, rr, :] = kd
            eg_s[d, er, :] = jnp.broadcast_to(eg, (8, width))
        return carry

    lax.fori_loop(0, s_len // tile, prep_tile, 0)

    st_s[...] = jnp.zeros(st_s.shape, F32)

    def scan_step(i, carry):
        chunks = (i, _backward_chunk(i, n_ctx_chunks, n_chunks))
        rrs = [pl.ds(pl.multiple_of(c * CHUNK, CHUNK), CHUNK) for c in chunks]
        states = [st_s[d] for d in range(2)]
        ws_qs = [_dot(jnp.concatenate([w_s[d, rrs[d], :], qg_s[d, rrs[d], :]], axis=0),
                      states[d].astype(BF16) * bdones) for d in range(2)]
        v_new = [u_s[d, rrs[d], :] - ws_qs[d][0:CHUNK] for d in range(2)]
        upd = [_dot_tn(kd_s[d, rrs[d], :], v_new[d].astype(BF16)) for d in range(2)]
        for d in range(2):
            eg = eg_s[d, pl.ds(pl.multiple_of(chunks[d] * 8, 8), 1), :]
            st_s[d] = states[d] * eg + upd[d]
        outs = [ws_qs[d][CHUNK:2 * CHUNK] + _dot(sc_s[d, rrs[d], :], _blockdiag(v_new[d], bdones)) for d in range(2)]
        for d in range(2):
            os_s[rrs[d], :] = os_s[rrs[d], :] + outs[d]
        return carry

    lax.fori_loop(0, n_chunks, scan_step, 0)

    def out_tile(t, carry):
        r0 = pl.multiple_of(t * 256, 256)
        o = os_s[pl.ds(r0, 256), :]
        ms = _dot((o * o).astype(BF16), bdones) * (1.0 / HEAD_DIM)
        gate = p_ref[0, pl.ds(r0, 256), 3 * width:4 * width]
        y = o * lax.rsqrt(ms + EPS) * ng_ref[...] * (gate * _sigmoid(gate))
        o_ref[0, pl.ds(r0, 256), :] = y.astype(o_ref.dtype)
        return carry

    lax.fori_loop(0, s_len // 256, out_tile, 0)


def _dn_call(p_dn, conv_w, a_log, dt_bias, norm_g, ctx_len):
    b, s, n = p_dn.shape
    width = GROUP_LANES
    lanes = lambda a: jnp.zeros((1, 128), F32).at[0, :a.size].set(a.astype(F32).reshape(-1))
    ng = jnp.tile(norm_g.astype(F32), width // HEAD_DIM).reshape(1, width)
    const = lambda shape: pl.BlockSpec(shape, lambda i: (0, 0))
    return pl.pallas_call(
        functools.partial(_dn_kernel, ctx_len=ctx_len),
        out_shape=jax.ShapeDtypeStruct((b, s, width), BF16),
        grid=(b,),
        in_specs=[pl.BlockSpec((1, s, n), lambda i: (i, 0, 0)),
                  const((CONV_W, 3 * width)), const((1, 128)), const((1, 128)), const((1, width))],
        out_specs=pl.BlockSpec((1, s, width), lambda i: (i, 0, 0)),
        scratch_shapes=[pltpu.VMEM((3 * width // 128, s + 24, 128), F32),
                        pltpu.VMEM((2, s, width), F32)] + [pltpu.VMEM((2, s, width), BF16)] * 4 + [
            pltpu.VMEM((2, s // CHUNK * 8, width), F32), pltpu.VMEM((s, width), F32),
            pltpu.VMEM((2, width, width), F32)],
        compiler_params=pltpu.CompilerParams(dimension_semantics=("arbitrary",), vmem_limit_bytes=VMEM_LIMIT),
        name="deltanet",
    )(p_dn, conv_w.astype(F32), lanes(a_log), lanes(dt_bias), ng)


def _hg_kernel(p_ref, lb_ref, ng_ref, o_ref, bc_s, kk_s, os_s, st_s, *, ctx_len):
    s_len = p_ref.shape[1]
    n_chunks = s_len // CHUNK
    n_ctx_chunks = ctx_len // CHUNK
    width = GROUP_LANES
    bdmask = _head_block_mask(width)
    bdones = _mask_bf16(bdmask)
    low, upp = _chunk_cumsum_mats()
    lb = jnp.concatenate([lb_ref[0:1, :], lb_ref[1:2, :]], axis=1)
    log_lb = jnp.log(lb)
    log_1m = jnp.log1p(-lb)

    def gate_tile(t, carry):
        r0 = pl.multiple_of(t * 256, 256)
        z = p_ref[0, pl.ds(r0, 256), width:3 * width]
        ez = jnp.exp(-jnp.abs(z))
        l1p = jnp.log1p(ez)
        tt = log_1m + (jnp.minimum(z, 0.0) - l1p)
        logf = jnp.maximum(log_lb, tt) + jnp.log1p(jnp.exp(-jnp.abs(log_lb - tt)))
        kk_s[pl.ds(r0, 256), :] = (1.0 - lb) * _sigmoid(-z)
        c0, c1 = _chunk_cumsums(logf, low, upp)
        bc_s[pl.ds(r0, 256), 0:width] = c0
        bc_s[pl.ds(r0, 256), width:2 * width] = c1
        os_s[pl.ds(r0, 256), :] = jnp.zeros((256, width), F32)
        return carry

    lax.fori_loop(0, s_len // 256, gate_tile, 0)

    st_s[...] = jnp.zeros(st_s.shape, F32)
    row = _iota((CHUNK, width), 0)
    col = _iota((CHUNK, width), 1) & 63
    sub8 = _iota((8, width), 0)
    blk16, blk4 = CHUNK // 16, 16 // 4

    codes = []
    for d in range(2):
        ahead16 = ((row >> 4) > (col >> 4)) if d == 0 else ((row >> 4) < (col >> 4))
        rp, cp = (row >> 2) & 3, (col >> 2) & 3
        ahead4 = ((row >> 4) == (col >> 4)) & ((rp > cp) if d == 0 else (rp < cp))
        delta = (row - col) if d == 0 else (col - row)
        codes.append((jnp.where(ahead16, col >> 4, -1), jnp.where(ahead4, cp, -1),
                      jnp.where(((row >> 2) == (col >> 2)) & (delta >= 0), delta, -1)))

    def score_operands(d, q, k, bcb):
        if d == 0:
            e16 = lambda j: bcb[16 * j + 15:16 * j + 16, :]
            e4 = lambda m: bcb[4 * m + 3:4 * m + 4, :]
            others = range(0, 3)
        else:
            e16 = lambda j: bcb[16 * j:16 * j + 1, :]
            e4 = lambda m: bcb[4 * m:4 * m + 1, :]
            others = range(1, 4)
        key_e16 = jnp.concatenate([jnp.broadcast_to(e16(j), (16, width)) for j in range(blk16)], axis=0)
        key_e4 = jnp.concatenate([jnp.where(sub8 < 4, e4(2 * g), e4(2 * g + 1)) for g in range(CHUNK // 8)], axis=0)
        k1 = _blockdiag(k * jnp.exp(key_e16 - bcb), bdones)
        k2 = _blockdiag(k * jnp.exp(key_e4 - bcb), bdones)
        q1 = jnp.concatenate([(q * jnp.exp(bcb - e16(j))).astype(BF16) for j in others], axis=0)
        q2 = []
        for p in others:
            edge = jnp.concatenate([jnp.broadcast_to(e4(blk4 * b + p), (16, width)) for b in range(blk16)], axis=0)
            q2.append((q * jnp.exp(bcb - edge)).astype(BF16))
        terms = [(q * k).astype(BF16)]
        for delta in range(1, 4):
            shift = delta if d == 0 else CHUNK - delta
            ks = pltpu.roll(k, shift, 0)
            bs = pltpu.roll(bcb, shift, 0)
            terms.append((q * ks * jnp.exp(bcb - bs)).astype(BF16))
        return q1, k1, jnp.concatenate(q2, axis=0), k2, jnp.concatenate(terms, axis=0), others

    def scan_step(i, carry):
        chunks = (i, _backward_chunk(i, n_ctx_chunks, n_chunks))
        rrs = [pl.ds(pl.multiple_of(c * CHUNK, CHUNK), CHUNK) for c in chunks]
        qs = [p_ref[0, rrs[d], 0:width] for d in range(2)]
        vs = [p_ref[0, rrs[d], 3 * width:4 * width] for d in range(2)]
        ks = [kk_s[rrs[d], d * width:(d + 1) * width] for d in range(2)]
        bcs = [bc_s[rrs[d], d * width:(d + 1) * width] for d in range(2)]
        b_last = [bcs[0][CHUNK - 1:CHUNK, :], bcs[1][0:1, :]]
        ops = [score_operands(d, qs[d], ks[d], bcs[d]) for d in range(2)]
        r1 = [_dot_nt(ops[d][0], ops[d][1]) for d in range(2)]
        r2 = [_dot_nt(ops[d][2], ops[d][3]) for d in range(2)]
        r3 = [_dot(ops[d][4], bdones) for d in range(2)]
        states = [st_s[d] for d in range(2)]
        upd = [_dot_tn(vs[d].astype(BF16), (ks[d] * jnp.exp(b_last[d] - bcs[d])).astype(BF16)) for d in range(2)]
        from_state = [_dot_nt((qs[d] * jnp.exp(bcs[d])).astype(BF16), states[d].astype(BF16) * bdones)
                      for d in range(2)]
        for d in range(2):
            st_s[d] = states[d] * jnp.exp(b_last[d]) + upd[d]
        outs = []
        for d in range(2):
            scores = jnp.zeros((CHUNK, width), F32)
            for n, j in enumerate(ops[d][5]):
                scores = jnp.where(codes[d][0] == j, r1[d][n * CHUNK:(n + 1) * CHUNK], scores)
                scores = jnp.where(codes[d][1] == j, r2[d][n * CHUNK:(n + 1) * CHUNK], scores)
            for delta in range(4):
                scores = jnp.where(codes[d][2] == delta, r3[d][delta * CHUNK:(delta + 1) * CHUNK], scores)
            outs.append(from_state[d] + _dot(scores.astype(BF16), _blockdiag(vs[d], bdones)))
        for d in range(2):
            os_s[rrs[d], :] = os_s[rrs[d], :] + outs[d]
        return carry

    lax.fori_loop(0, n_chunks, scan_step, 0)

    def out_tile(t, carry):
        r0 = pl.multiple_of(t * 256, 256)
        o = os_s[pl.ds(r0, 256), :]
        ms = _dot((o * o).astype(BF16), bdones) * (1.0 / HEAD_DIM)
        gate = p_ref[0, pl.ds(r0, 256), 4 * width:5 * width]
        y = o * lax.rsqrt(ms + EPS) * ng_ref[...] * (gate * _sigmoid(gate))
        o_ref[0, pl.ds(r0, 256), :] = y.astype(o_ref.dtype)
        return carry

    lax.fori_loop(0, s_len // 256, out_tile, 0)


def _hg_call(p_hg, lb_l, norm_g, ctx_len):
    b, s, n = p_hg.shape
    width = GROUP_LANES
    ng = jnp.tile(norm_g.astype(F32), width // HEAD_DIM).reshape(1, width)
    const = lambda shape: pl.BlockSpec(shape, lambda i: (0, 0))
    return pl.pallas_call(
        functools.partial(_hg_kernel, ctx_len=ctx_len),
        out_shape=jax.ShapeDtypeStruct((b, s, width), BF16),
        grid=(b,),
        in_specs=[pl.BlockSpec((1, s, n), lambda i: (i, 0, 0)), const((2, width)), const((1, width))],
        out_specs=pl.BlockSpec((1, s, width), lambda i: (i, 0, 0)),
        scratch_shapes=[pltpu.VMEM((s, 2 * width), F32), pltpu.VMEM((s, 2 * width), F32),
                        pltpu.VMEM((s, width), F32), pltpu.VMEM((2, width, width), F32)],
        compiler_params=pltpu.CompilerParams(dimension_semantics=("arbitrary",), vmem_limit_bytes=VMEM_LIMIT),
        name="hgrn2",
    )(p_hg, lb_l, ng)


def _swa_kernel(sink_ref, p_ref, cos_ref, sin_ref, o_ref, q_s, k_s, v_s, *, ctx_len):
    s_len = p_ref.shape[1]
    t_len = s_len - ctx_len
    blk = SWA_BLOCK
    qw = SWA_HEADS * HEAD_DIM
    kvw = SWA_KV_HEADS * HEAD_DIM
    group = SWA_HEADS // SWA_KV_HEADS
    gw = group * HEAD_DIM
    scale = HEAD_DIM ** -0.5
    half = HEAD_DIM // 2
    lane128 = _iota((blk, kvw), 1)
    first_half_q = (_iota((blk, qw), 1) & (HEAD_DIM - 1)) < half
    first_half_k = (lane128 & (HEAD_DIM - 1)) < half
    low_head = lane128 < HEAD_DIM

    def kv_tiles(x):
        sw = pltpu.roll(x, HEAD_DIM, 1)
        return jnp.where(low_head, x, sw), jnp.where(low_head, sw, x)

    def prep_tile(t, carry):
        r0 = pl.multiple_of(t * blk, blk)
        q = p_ref[0, pl.ds(r0, blk), 0:qw]
        k = p_ref[0, pl.ds(r0, blk), qw:qw + kvw]
        v = p_ref[0, pl.ds(r0, blk), qw + kvw:qw + 2 * kvw]
        pos = pl.multiple_of(jnp.maximum(r0 - ctx_len, 0), blk)
        cs = cos_ref[pl.ds(pos, blk), :]
        sn = sin_ref[pl.ds(pos, blk), :]
        is_ctx = r0 < ctx_len
        cs = jnp.where(is_ctx, 1.0, cs)
        sn = jnp.where(is_ctx, 0.0, sn)
        q_sw = jnp.where(first_half_q, pltpu.roll(q, qw - half, 1), pltpu.roll(q, half, 1))
        k_sw = jnp.where(first_half_k, pltpu.roll(k, kvw - half, 1), pltpu.roll(k, half, 1))
        cs4 = jnp.concatenate([cs] * (qw // kvw), axis=1)
        sn4 = jnp.concatenate([sn] * (qw // kvw), axis=1)
        q_s[pl.ds(r0, blk), :] = ((q * cs4 + q_sw * sn4) * scale).astype(BF16)
        k0, k1 = kv_tiles(k * cs + k_sw * sn)
        v0, v1 = kv_tiles(v)
        k_s[0, pl.ds(r0, blk), :] = k0.astype(BF16)
        k_s[1, pl.ds(r0, blk), :] = k1.astype(BF16)
        v_s[0, pl.ds(r0, blk), :] = v0.astype(BF16)
        v_s[1, pl.ds(r0, blk), :] = v1.astype(BF16)
        return carry

    lax.fori_loop(0, s_len // blk, prep_tile, 0)

    rows = group * blk
    blk_shift = blk.bit_length() - 1
    qmask = _mask_bf16((_iota((rows, gw), 0) >> blk_shift) == (_iota((rows, gw), 1) >> 6))
    rb1 = _iota((rows, 1), 0) >> blk_shift
    r_in = _iota((rows, 3 * blk), 0) & (blk - 1)
    c_in = _iota((rows, 3 * blk), 1)
    lane_g = _iota((blk, gw), 1) >> 6

    def wide(x):
        return jnp.concatenate([x, x], axis=1)

    def attend(r0, local_start):
        gs = range(SWA_KV_HEADS)
        qst, sink = [], []
        for g in gs:
            qb = q_s[pl.ds(r0, blk), g * gw:(g + 1) * gw]
            qst.append(jnp.concatenate([qb] * group, axis=0) * qmask)
            sk = jnp.zeros((rows, 1), F32)
            for j in range(group):
                sk = jnp.where(rb1 == j, sink_ref[g * group + j], sk)
            sink.append(sk)
        s_ctx = [_dot_nt(qst[g], wide(k_s[g, 0:ctx_len, :])) for g in gs]
        m = [jnp.maximum(jnp.max(s_ctx[g], axis=-1, keepdims=True), sink[g]) for g in gs]
        if local_start is not None:
            valid = jnp.abs((r0 - ctx_len) + r_in - (local_start + c_in)) <= WINDOW
            loc = pl.ds(ctx_len + local_start, 3 * blk)
            s_loc = [jnp.where(valid, _dot_nt(qst[g], wide(k_s[g, loc, :])), NEG_INF) for g in gs]
            m = [jnp.maximum(m[g], jnp.max(s_loc[g], axis=-1, keepdims=True)) for g in gs]
            e_loc = [jnp.exp(s_loc[g] - m[g]) for g in gs]
        e_ctx = [jnp.exp(s_ctx[g] - m[g]) for g in gs]
        den = [jnp.sum(e_ctx[g], axis=-1, keepdims=True) + jnp.exp(sink[g] - m[g]) for g in gs]
        acc = [_dot(e_ctx[g].astype(BF16), wide(v_s[g, 0:ctx_len, :])) for g in gs]
        if local_start is not None:
            den = [den[g] + jnp.sum(e_loc[g], axis=-1, keepdims=True) for g in gs]
            acc = [acc[g] + _dot(e_loc[g].astype(BF16), wide(v_s[g, loc, :])) for g in gs]
        for g in gs:
            a = acc[g] / den[g]
            out = jnp.zeros((blk, gw), F32)
            for j in range(group):
                out = jnp.where(lane_g == j, a[j * blk:(j + 1) * blk], out)
            o_ref[0, pl.ds(r0, blk), g * gw:(g + 1) * gw] = out.astype(o_ref.dtype)

    def ctx_block(t, carry):
        attend(pl.multiple_of(t * blk, blk), None)
        return carry

    lax.fori_loop(0, ctx_len // blk, ctx_block, 0)

    def lat_block(t, carry):
        r0 = pl.multiple_of(ctx_len + t * blk, blk)
        attend(r0, pl.multiple_of(jnp.clip((t - 1) * blk, 0, t_len - 3 * blk), blk))
        return carry

    lax.fori_loop(0, t_len // blk, lat_block, 0)


def _swa_call(p_sw, sink, cos_t, sin_t, ctx_len):
    b, s, n = p_sw.shape
    qw = SWA_HEADS * HEAD_DIM
    kvw = SWA_KV_HEADS * HEAD_DIM
    t_len = s - ctx_len
    return pl.pallas_call(
        functools.partial(_swa_kernel, ctx_len=ctx_len),
        out_shape=jax.ShapeDtypeStruct((b, s, qw), BF16),
        grid=(b,),
        in_specs=[pl.BlockSpec(memory_space=pltpu.SMEM),
                  pl.BlockSpec((1, s, n), lambda i: (i, 0, 0)),
                  pl.BlockSpec((t_len, kvw), lambda i: (0, 0)),
                  pl.BlockSpec((t_len, kvw), lambda i: (0, 0))],
        out_specs=pl.BlockSpec((1, s, qw), lambda i: (i, 0, 0)),
        scratch_shapes=[pltpu.VMEM((s, qw), BF16), pltpu.VMEM((SWA_KV_HEADS, s, kvw), BF16),
                        pltpu.VMEM((SWA_KV_HEADS, s, kvw), BF16)],
        compiler_params=pltpu.CompilerParams(dimension_semantics=("arbitrary",), vmem_limit_bytes=VMEM_LIMIT),
        name="swa",
    )(sink.astype(F32), p_sw, cos_t, sin_t)


def _rope_tables(t_len):
    rows = t_len // GRID_W
    row = jnp.repeat(jnp.arange(rows), GRID_W).astype(F32)
    col = jnp.tile(jnp.arange(GRID_W), rows).astype(F32)
    half = HEAD_DIM // 2
    inv = ROPE_BASE ** (-jnp.arange(0, half, 2, dtype=F32) / half)
    ang = jnp.concatenate([row[:, None] * inv, col[:, None] * inv], axis=-1)
    cos, sin = jnp.cos(ang), jnp.sin(ang)
    return jnp.tile(cos, (1, 4)), jnp.tile(jnp.concatenate([-sin, sin], axis=-1), (1, 2))


def kernel(x, c, ctx, c_ctx, w_ada, b_ada, norm1, norm2, w_in, dn_conv, dn_A_log, dn_dt_bias, dn_norm, swa_sink,
           hg_lb_logits, hg_norm, w_out, w_ff1, w_ff2, norm_f):
    b, t_len, d = x.shape
    ctx_len = ctx.shape[1]
    depth = w_ada.shape[0]
    tm = 256
    assert b + 1 <= MOD_ROWS and ctx_len % tm == 0 and t_len % tm == 0 and t_len >= 3 * SWA_BLOCK

    cc = jnp.zeros((MOD_ROWS, d), F32).at[:b].set(c.astype(F32)).at[b].set(c_ctx.astype(F32))
    mods = _ada_call(cc, w_ada.astype(F32), b_ada.astype(F32))
    lb_all = _lb_call(hg_lb_logits)
    cos_t, sin_t = _rope_tables(t_len)

    n_head = 4 * GROUP_LANES + 16
    w_in_p = jnp.concatenate(
        [w_in[:, :, :n_head], jnp.zeros((depth, d, 128 - 16), w_in.dtype), w_in[:, :, n_head:]], axis=-1).astype(BF16)
    w_out_b = w_out.astype(BF16)
    w1_b = w_ff1.astype(BF16)
    w2_b = w_ff2.astype(BF16)

    xs = jnp.concatenate([ctx.astype(F32), x.astype(F32)], axis=1)
    for l in range(depth):
        last = l == depth - 1
        p_dn, p_sw, p_hg = _in_proj_call(xs, mods[l], norm1[l].astype(F32), w_in_p[l], ctx_len, tm)
        y_dn = _dn_call(p_dn, dn_conv[l], dn_A_log[l], dn_dt_bias[l], dn_norm[l], ctx_len)
        y_sw = _swa_call(p_sw, swa_sink[l], cos_t, sin_t, ctx_len)
        y_hg = _hg_call(p_hg, lb_all[:, l], hg_norm[l], ctx_len)
        xs = _out_mlp_call(xs, y_dn, y_sw, y_hg, mods[l], norm2[l].astype(F32), norm_f.astype(F32),
                           w_out_b[l], w1_b[l], w2_b[l], ctx_len, tm, latent_only=last)
    return xs.astype(x.dtype)
```

```python
import functools
import math

import jax
import jax.numpy as jnp
from jax import lax
from jax.experimental import pallas as pl
from jax.experimental.pallas import tpu as pltpu

F32 = jnp.float32
BF16 = jnp.bfloat16

HEAD_DIM = 64
GROUP_LANES = 256
CHUNK = 64
N_MOD = 6
EPS = 1e-6
CONV_W = 5
SWA_HEADS = 8
SWA_KV_HEADS = 2
SWA_BLOCK = 128
WINDOW = 128
GRID_W = 64
ROPE_BASE = 10000.0
MOD_ROWS = 24
MOD_SUB = 256
VMEM_LIMIT = 56 * 1024 * 1024
NEG_INF = float("-inf")
LOG2E = 1.0 / math.log(2.0)


def _dot(a, b):
    return jnp.dot(a, b, preferred_element_type=F32)


def _dot_nt(a, b):
    return lax.dot_general(a, b, (((1,), (1,)), ((), ())), preferred_element_type=F32)


def _dot_tn(a, b):
    return lax.dot_general(a, b, (((0,), (0,)), ((), ())), preferred_element_type=F32)


def _split(x):
    hi = x.astype(BF16)
    lo = (x - hi.astype(F32)).astype(BF16)
    return hi, lo


def _iota(shape, dim):
    return lax.broadcasted_iota(jnp.int32, shape, dim)


def _sigmoid(x):
    return 1.0 / (1.0 + jnp.exp(-x))


def _softplus(x):
    return jnp.maximum(x, 0.0) + jnp.log1p(jnp.exp(-jnp.abs(x)))


def _head_block_mask(n):
    return (_iota((n, n), 0) >> 6) == (_iota((n, n), 1) >> 6)


def _mask_bf16(cond):
    return jnp.where(cond, 1.0, 0.0).astype(BF16)


def _blockdiag(x, ones_bd):
    xb = x.astype(BF16)
    return jnp.concatenate([xb, xb, xb, xb], axis=0) * ones_bd


def _chunk_cumsum_mats():
    r = _iota((256, 512), 0)
    c = _iota((256, 512), 1) & 255
    same = (r >> 6) == (c >> 6)
    return _mask_bf16(same & (c <= r)), _mask_bf16(same & (c >= r))


def _chunk_cumsums(x, low, upp):
    hi, lo = _split(x)
    c0 = _dot(low, jnp.concatenate([hi[:, :256], lo[:, :256]], axis=0))
    c1 = _dot(upp, jnp.concatenate([hi[:, 256:], lo[:, 256:]], axis=0))
    return c0, c1


def _backward_chunk(i, n_ctx_chunks, n_chunks):
    return jnp.where(i < n_ctx_chunks, n_ctx_chunks - 1 - i, n_chunks - 1 + n_ctx_chunks - i)


def _ada_kernel(c_ref, w_ref, b_ref, o_ref):
    c = c_ref[...]
    a = c * _sigmoid(c)
    a_hi, a_lo = _split(a)
    w = w_ref[0]
    w_hi, w_lo = _split(w)
    acc = _dot(a_hi, w_hi) + _dot(a_lo, w_hi) + _dot(a_hi, w_lo)
    o_ref[0] = acc + b_ref[0]


def _ada_call(cc, w_ada, b_ada):
    depth, d, n = w_ada.shape
    tn = 1536
    return pl.pallas_call(
        _ada_kernel,
        out_shape=jax.ShapeDtypeStruct((depth, MOD_ROWS, n), F32),
        grid=(depth, n // tn),
        in_specs=[
            pl.BlockSpec((MOD_ROWS, d), lambda l, j: (0, 0)),
            pl.BlockSpec((1, d, tn), lambda l, j: (l, 0, j)),
            pl.BlockSpec((1, 1, tn), lambda l, j: (l, 0, j)),
        ],
        out_specs=pl.BlockSpec((1, MOD_ROWS, tn), lambda l, j: (l, 0, j)),
        compiler_params=pltpu.CompilerParams(
            dimension_semantics=("arbitrary", "arbitrary"), vmem_limit_bytes=VMEM_LIMIT),
        name="ada_mod",
    )(cc, w_ada, b_ada.reshape(depth, 1, n))


def _lb_kernel(x_ref, o_ref):
    depth = x_ref.shape[1]
    rows = [x_ref[:, l, :] for l in range(depth)]
    m = rows[0]
    for r in rows[1:]:
        m = jnp.maximum(m, r)
    es = [jnp.exp(r - m) for r in rows]
    tot = es[0]
    for e in es[1:]:
        tot = tot + e
    ps = [e / tot for e in es]
    run = ps[0]
    o_ref[:, 0, :] = run - ps[0]
    for l in range(1, depth):
        run = run + ps[l]
        o_ref[:, l, :] = run - ps[0]


def _lb_call(logits):
    return pl.pallas_call(
        _lb_kernel,
        out_shape=jax.ShapeDtypeStruct(logits.shape, F32),
        name="hg_lower_bounds",
    )(logits.astype(F32))


def _mod_row(mods_ref, row, idx, d):
    return mods_ref[pl.ds(row, 1), idx * d:(idx + 1) * d]


def _rms(x):
    return x * lax.rsqrt(jnp.mean(x * x, axis=-1, keepdims=True) + EPS)


def _mod_rows(ctx_subs, n_batch, tile_offset, n_sub):
    first = (pl.program_id(1) + tile_offset) * n_sub
    return [jnp.where(first + i < ctx_subs, n_batch, pl.program_id(0)) for i in range(n_sub)]


def _in_proj_kernel(x_ref, mods_ref, g_ref, w_ref, dn_ref, sw_ref, hg_ref, *, ctx_subs, n_batch):
    d = x_ref.shape[-1]
    n_sub = x_ref.shape[1] // MOD_SUB
    hs = []
    for i, row in enumerate(_mod_rows(ctx_subs, n_batch, 0, n_sub)):
        x = x_ref[0, i * MOD_SUB:(i + 1) * MOD_SUB, :]
        h = _rms(x) * g_ref[...] * (1.0 + _mod_row(mods_ref, row, 1, d)) + _mod_row(mods_ref, row, 0, d)
        hs.append(h.astype(BF16))
    p = _dot(jnp.concatenate(hs, axis=0), w_ref[...])
    n_dn = dn_ref.shape[-1]
    n_sw = sw_ref.shape[-1]
    dn_ref[0] = p[:, :n_dn]
    sw_ref[0] = p[:, n_dn:n_dn + n_sw]
    hg_ref[0] = p[:, n_dn + n_sw:]


def _in_proj_call(x, mods, g1, w_in_p, layer, ctx_len, tm):
    b, s, d = x.shape
    n_dn, n_sw, n_hg = 1152, 768, 1280
    n_all = w_in_p.shape[-1]
    kern = functools.partial(_in_proj_kernel, ctx_subs=ctx_len // MOD_SUB, n_batch=b)
    return pl.pallas_call(
        kern,
        out_shape=(jax.ShapeDtypeStruct((b, s, n_dn), F32),
                   jax.ShapeDtypeStruct((b, s, n_sw), F32),
                   jax.ShapeDtypeStruct((b, s, n_hg), F32)),
        grid=(b, s // tm),
        in_specs=[
            pl.BlockSpec((1, tm, d), lambda i, t: (i, t, 0)),
            pl.BlockSpec((None,) + mods.shape[1:], lambda i, t: (layer, 0, 0)),
            pl.BlockSpec((1, d), lambda i, t: (0, 0)),
            pl.BlockSpec((None, d, n_all), lambda i, t: (layer, 0, 0), pipeline_mode=pl.Buffered(1)),
        ],
        out_specs=(pl.BlockSpec((1, tm, n_dn), lambda i, t: (i, t, 0)),
                   pl.BlockSpec((1, tm, n_sw), lambda i, t: (i, t, 0)),
                   pl.BlockSpec((1, tm, n_hg), lambda i, t: (i, t, 0))),
        compiler_params=pltpu.CompilerParams(
            dimension_semantics=("arbitrary", "arbitrary"), vmem_limit_bytes=VMEM_LIMIT),
        name="in_proj",
    )(x, mods, g1.reshape(1, d), w_in_p)


def _out_mlp_kernel(x_ref, dn_ref, sw_ref, hg_ref, mods_ref, g2_ref, gf_ref, wo_ref, w1_ref, w2_ref, o_ref,
                    *, ctx_subs, tile_offset, n_batch, final_norm):
    d = x_ref.shape[-1]
    n_sub = x_ref.shape[1] // MOD_SUB
    rows = _mod_rows(ctx_subs, n_batch, tile_offset, n_sub)
    subs = [slice(i * MOD_SUB, (i + 1) * MOD_SUB) for i in range(n_sub)]
    y = jnp.concatenate([dn_ref[0], sw_ref[0], hg_ref[0]], axis=-1)
    mix = _dot(y, wo_ref[...])
    x1 = [x_ref[0, s, :] + _mod_row(mods_ref, r, 2, d) * mix[s] for r, s in zip(rows, subs)]
    hb = jnp.concatenate(
        [(_rms(xs) * g2_ref[...] * (1.0 + _mod_row(mods_ref, r, 4, d)) + _mod_row(mods_ref, r, 3, d)).astype(BF16)
         for r, xs in zip(rows, x1)], axis=0)
    d_ff = w1_ref.shape[1]
    ff_chunk = 1024
    acc = jnp.zeros(mix.shape, F32)
    for j in range(d_ff // ff_chunk):
        a = _dot(hb, w1_ref[:, j * ff_chunk:(j + 1) * ff_chunk])
        a = jnp.square(jnp.maximum(a, 0.0))
        acc = acc + _dot(a.astype(BF16), w2_ref[j * ff_chunk:(j + 1) * ff_chunk, :])
    for r, s, xs in zip(rows, subs, x1):
        x2 = xs + _mod_row(mods_ref, r, 5, d) * acc[s]
        if final_norm:
            x2 = _rms(x2) * gf_ref[...]
        o_ref[0, s, :] = x2


def _out_mlp_call(x, y_dn, y_sw, y_hg, mods, g2, gf, w_out, w1, w2, layer, ctx_len, tm, latent_only):
    b, s, d = x.shape
    d_ff = w1.shape[-1]
    off = ctx_len // tm if latent_only else 0
    s_out = s - ctx_len if latent_only else s
    kern = functools.partial(_out_mlp_kernel, ctx_subs=ctx_len // MOD_SUB, tile_offset=off, n_batch=b,
                             final_norm=latent_only)
    tok = lambda w: pl.BlockSpec((1, tm, w), lambda i, t: (i, t + off, 0))
    const = lambda shape: pl.BlockSpec(shape, lambda i, t: (0, 0))
    per_layer = lambda shape, **kw: pl.BlockSpec((None,) + shape, lambda i, t: (layer, 0, 0), **kw)
    weight = lambda shape: per_layer(shape, pipeline_mode=pl.Buffered(1))
    return pl.pallas_call(
        kern,
        out_shape=jax.ShapeDtypeStruct((b, s_out, d), F32),
        grid=(b, s_out // tm),
        in_specs=[tok(d), tok(y_dn.shape[-1]), tok(y_sw.shape[-1]), tok(y_hg.shape[-1]),
                  per_layer(mods.shape[1:]), const((1, d)), const((1, d)),
                  weight((d, d)), weight((d, d_ff)), weight((d_ff, d))],
        out_specs=pl.BlockSpec((1, tm, d), lambda i, t: (i, t, 0)),
        compiler_params=pltpu.CompilerParams(
            dimension_semantics=("arbitrary", "arbitrary"), vmem_limit_bytes=VMEM_LIMIT),
        name="out_mlp",
    )(x, y_dn, y_sw, y_hg, mods, g2.reshape(1, d), gf.reshape(1, d), w_out, w1, w2)


def _dn_kernel(p_ref, cw_ref, alog_ref, dtb_ref, ng_ref, o_ref, xs_s, u_s, w_s, sc_s, qg_s, kd_s, eg_s, os_s, st_s,
               *, ctx_len):
    s_len = p_ref.shape[1]
    n_chunks = s_len // CHUNK
    n_ctx_chunks = ctx_len // CHUNK
    width = GROUP_LANES
    sub = 4 * CHUNK
    tile = sub
    bdmask = _head_block_mask(width)
    bdones = _mask_bf16(bdmask)
    low, upp = _chunk_cumsum_mats()
    expand = _mask_bf16((_iota((256, 1024), 1) >> 6) == (_iota((256, 1024), 0) & 127))
    row = _iota((CHUNK, width), 0)
    col = _iota((CHUNK, width), 1) & 63
    eye = jnp.where(row == col, 1.0, 0.0)

    def solve_chunks(chains):
        n = len(chains)
        a_mats, scs, rhss, tails = [], [], [], []
        for d, q, k, v, bt, gcb in chains:
            if d == 0:
                incl, strict = col <= row, col < row
                gtot = gcb[CHUNK - 1:CHUNK, :]
            else:
                incl, strict = col >= row, col > row
                gtot = gcb[0:1, :]
            grow = jnp.sum(gcb * eye, axis=0, keepdims=True)
            decay = jnp.exp2(jnp.where(incl, gcb - grow, NEG_INF))
            kb = k * bt
            egc = jnp.exp2(gcb)
            kq = jnp.concatenate([kb, q], axis=0).astype(BF16)
            raw = _dot_nt(kq, _blockdiag(k, bdones))
            a_mats.append(jnp.where(strict, raw[0:CHUNK] * decay, 0.0))
            scs.append((raw[CHUNK:2 * CHUNK] * decay).astype(BF16))
            rhss.append(jnp.concatenate([_blockdiag(v * bt, bdones), _blockdiag(kb * egc, bdones)], axis=1))
            tails.append(((q * egc).astype(BF16), (k * jnp.exp2(gtot - gcb)).astype(BF16), jnp.exp2(gtot)))
        ps = [eye - jnp.where((row >> 1) == (col >> 1), a, 0.0) for a in a_mats]
        for lg in range(1, 6):
            pair = ((row >> (lg + 1)) == (col >> (lg + 1))) & ((row >> lg) != (col >> lg))
            cts = [_dot(jnp.where(pair, a_mats[i], 0.0).astype(BF16), _blockdiag(ps[i], bdones)) for i in range(n)]
            ps = [ps[i] - _dot(ps[i].astype(BF16), _blockdiag(cts[i], bdones)) for i in range(n)]
        outs = []
        for i in range(n):
            uw = _dot(ps[i].astype(BF16), rhss[i])
            outs.append((uw[:, 0:width], uw[:, width:2 * width].astype(BF16), scs[i]) + tails[i])
        return outs

    gap = 8
    n_slabs = 3 * width // 128
    zeros_gap = jnp.zeros((n_slabs, gap, 128), F32)
    xs_s[:, 0:gap, :] = zeros_gap
    xs_s[:, gap + ctx_len:2 * gap + ctx_len, :] = zeros_gap
    xs_s[:, 2 * gap + s_len:3 * gap + s_len, :] = zeros_gap

    def padded_row(r0):
        return r0 + gap + jnp.where(r0 >= ctx_len, gap, 0)

    def copy_tile(t, carry):
        r0 = pl.multiple_of(t * sub, sub)
        dst = pl.ds(pl.multiple_of(padded_row(r0), 8), sub)
        for c in range(n_slabs):
            xs_s[c, dst, :] = p_ref[0, pl.ds(r0, sub), c * 128:(c + 1) * 128]
        return carry

    lax.fori_loop(0, s_len // sub, copy_tile, 0)
    lane_ab = _iota((sub, 128), 1)

    def conv_tap(base, j):
        rows = pl.ds(base + j, sub)
        return jnp.concatenate([xs_s[c, rows, :] for c in range(n_slabs)], axis=1) * cw_ref[j:j + 1, :]

    def sub_tile_chains(r0):
        base = padded_row(r0) - CONV_W // 2
        acc = conv_tap(base, 0)
        for j in range(1, CONV_W):
            acc = acc + conv_tap(base, j)
        y = acc * _sigmoid(acc)
        q = y[:, 0:width]
        k = y[:, width:2 * width]
        v = y[:, 2 * width:3 * width]
        q = q * lax.rsqrt(_dot((q * q).astype(BF16), bdones) + EPS) * (HEAD_DIM ** -0.5)
        k = k * lax.rsqrt(_dot((k * k).astype(BF16), bdones) + EPS)
        ab = p_ref[0, pl.ds(r0, sub), 4 * width:4 * width + 128]
        narrow = jnp.where(lane_ab < 8, (-LOG2E) * jnp.exp(alog_ref[...]) * _softplus(ab + dtb_ref[...]),
                           _sigmoid(ab))
        hi, lo = _split(narrow)
        abx = _dot(jnp.concatenate([hi, lo], axis=1), expand)
        cums = _chunk_cumsums(abx[:, 0:2 * width], low, upp)
        beta = abx[:, 2 * width:4 * width]
        chains = []
        for cc in range(sub // CHUNK):
            rs = slice(cc * CHUNK, (cc + 1) * CHUNK)
            for d in range(2):
                chains.append((d, q[rs], k[rs], v[rs], beta[rs, d * width:(d + 1) * width], cums[d][rs]))
        return chains

    def prep_tile(t, carry):
        r0 = pl.multiple_of(t * tile, tile)
        os_s[pl.ds(r0, tile), :] = jnp.zeros((tile, width), F32)
        chains = []
        for sb in range(tile // sub):
            chains += sub_tile_chains(pl.multiple_of(r0 + sb * sub, sub))
        for idx, (u, w, sc, qg, kd, eg) in enumerate(solve_chunks(chains)):
            cc, d = idx // 2, idx % 2
            rr = pl.ds(r0 + cc * CHUNK, CHUNK)
            er = pl.ds(pl.multiple_of((t * (tile // CHUNK) + cc) * 8, 8), 8)
            u_s[d, rr, :] = u
            w_s[d, rr, :] = w
            sc_s[d, rr, :] = sc
            qg_s[d, rr, :] = qg
            kd_s[d, rr, :] = kd
            eg_s[d, er, :] = jnp.broadcast_to(eg, (8, width))
        return carry

    lax.fori_loop(0, s_len // tile, prep_tile, 0)

    st_s[...] = jnp.zeros(st_s.shape, F32)

    def scan_step(i, carry):
        chunks = (i, _backward_chunk(i, n_ctx_chunks, n_chunks))
        rrs = [pl.ds(pl.multiple_of(c * CHUNK, CHUNK), CHUNK) for c in chunks]
        states = [st_s[d] for d in range(2)]
        ws_qs = [_dot(jnp.concatenate([w_s[d, rrs[d], :], qg_s[d, rrs[d], :]], axis=0),
                      states[d].astype(BF16) * bdones) for d in range(2)]
        v_new = [u_s[d, rrs[d], :] - ws_qs[d][0:CHUNK] for d in range(2)]
        upd = [_dot_tn(kd_s[d, rrs[d], :], v_new[d].astype(BF16)) for d in range(2)]
        for d in range(2):
            eg = eg_s[d, pl.ds(pl.multiple_of(chunks[d] * 8, 8), 1), :]
            st_s[d] = states[d] * eg + upd[d]
        outs = [ws_qs[d][CHUNK:2 * CHUNK] + _dot(sc_s[d, rrs[d], :], _blockdiag(v_new[d], bdones)) for d in range(2)]
        for d in range(2):
            os_s[rrs[d], :] = os_s[rrs[d], :] + outs[d]
        return carry

    lax.fori_loop(0, n_chunks, scan_step, 0)

    def out_tile(t, carry):
        r0 = pl.multiple_of(t * 256, 256)
        o = os_s[pl.ds(r0, 256), :]
        ms = _dot((o * o).astype(BF16), bdones) * (1.0 / HEAD_DIM)
        gate = p_ref[0, pl.ds(r0, 256), 3 * width:4 * width]
        y = o * lax.rsqrt(ms + EPS) * ng_ref[...] * (gate * _sigmoid(gate))
        o_ref[0, pl.ds(r0, 256), :] = y.astype(o_ref.dtype)
        return carry

    lax.fori_loop(0, s_len // 256, out_tile, 0)


def _dn_call(p_dn, conv_w, a_log, dt_bias, norm_g, ctx_len):
    b, s, n = p_dn.shape
    width = GROUP_LANES
    lanes = lambda a: jnp.zeros((1, 128), F32).at[0, :a.size].set(a.astype(F32).reshape(-1))
    ng = jnp.tile(norm_g.astype(F32), width // HEAD_DIM).reshape(1, width)
    const = lambda shape: pl.BlockSpec(shape, lambda i: (0, 0))
    return pl.pallas_call(
        functools.partial(_dn_kernel, ctx_len=ctx_len),
        out_shape=jax.ShapeDtypeStruct((b, s, width), BF16),
        grid=(b,),
        in_specs=[pl.BlockSpec((1, s, n), lambda i: (i, 0, 0)),
                  const((CONV_W, 3 * width)), const((1, 128)), const((1, 128)), const((1, width))],
        out_specs=pl.BlockSpec((1, s, width), lambda i: (i, 0, 0)),
        scratch_shapes=[pltpu.VMEM((3 * width // 128, s + 24, 128), F32),
                        pltpu.VMEM((2, s, width), F32)] + [pltpu.VMEM((2, s, width), BF16)] * 4 + [
            pltpu.VMEM((2, s // CHUNK * 8, width), F32), pltpu.VMEM((s, width), F32),
            pltpu.VMEM((2, width, width), F32)],
        compiler_params=pltpu.CompilerParams(dimension_semantics=("arbitrary",), vmem_limit_bytes=VMEM_LIMIT),
        name="deltanet",
    )(p_dn, conv_w.astype(F32), lanes(a_log), lanes(dt_bias), ng)


def _hg_kernel(p_ref, lb_ref, ng_ref, o_ref, bc_s, kk_s, os_s, st_s, *, ctx_len):
    s_len = p_ref.shape[1]
    n_chunks = s_len // CHUNK
    n_ctx_chunks = ctx_len // CHUNK
    width = GROUP_LANES
    bdmask = _head_block_mask(width)
    bdones = _mask_bf16(bdmask)
    low, upp = _chunk_cumsum_mats()
    lb = jnp.concatenate([lb_ref[0:1, :], lb_ref[1:2, :]], axis=1)
    log_lb = jnp.log(lb)
    log_1m = jnp.log1p(-lb)

    def gate_tile(t, carry):
        r0 = pl.multiple_of(t * 256, 256)
        z = p_ref[0, pl.ds(r0, 256), width:3 * width]
        ez = jnp.exp(-jnp.abs(z))
        one_p = 1.0 + ez
        tt = log_1m + (jnp.minimum(z, 0.0) - jnp.log(one_p))
        logf = jnp.maximum(log_lb, tt) + jnp.log(1.0 + jnp.exp(-jnp.abs(log_lb - tt)))
        kk_s[pl.ds(r0, 256), :] = (1.0 - lb) * jnp.where(z > 0.0, ez, 1.0) / one_p
        c0, c1 = _chunk_cumsums(logf * LOG2E, low, upp)
        bc_s[pl.ds(r0, 256), 0:width] = c0
        bc_s[pl.ds(r0, 256), width:2 * width] = c1
        os_s[pl.ds(r0, 256), :] = jnp.zeros((256, width), F32)
        return carry

    lax.fori_loop(0, s_len // 256, gate_tile, 0)

    st_s[...] = jnp.zeros(st_s.shape, F32)
    row = _iota((CHUNK, width), 0)
    col = _iota((CHUNK, width), 1) & 63
    sub8 = _iota((8, width), 0)
    blk16, blk4 = CHUNK // 16, 16 // 4

    codes = []
    for d in range(2):
        ahead16 = ((row >> 4) > (col >> 4)) if d == 0 else ((row >> 4) < (col >> 4))
        rp, cp = (row >> 2) & 3, (col >> 2) & 3
        ahead4 = ((row >> 4) == (col >> 4)) & ((rp > cp) if d == 0 else (rp < cp))
        delta = (row - col) if d == 0 else (col - row)
        codes.append((jnp.where(ahead16, col >> 4, -1), jnp.where(ahead4, cp, -1),
                      jnp.where(((row >> 2) == (col >> 2)) & (delta >= 0), delta, -1)))

    def score_operands(d, q, k, bcb):
        if d == 0:
            e16 = lambda j: bcb[16 * j + 15:16 * j + 16, :]
            e4 = lambda m: bcb[4 * m + 3:4 * m + 4, :]
            others = range(0, 3)
        else:
            e16 = lambda j: bcb[16 * j:16 * j + 1, :]
            e4 = lambda m: bcb[4 * m:4 * m + 1, :]
            others = range(1, 4)
        key_e16 = jnp.concatenate([jnp.broadcast_to(e16(j), (16, width)) for j in range(blk16)], axis=0)
        key_e4 = jnp.concatenate([jnp.where(sub8 < 4, e4(2 * g), e4(2 * g + 1)) for g in range(CHUNK // 8)], axis=0)
        k1 = _blockdiag(k * jnp.exp2(key_e16 - bcb), bdones)
        k2 = _blockdiag(k * jnp.exp2(key_e4 - bcb), bdones)
        q1 = jnp.concatenate([(q * jnp.exp2(bcb - e16(j))).astype(BF16) for j in others], axis=0)
        q2 = []
        for p in others:
            edge = jnp.concatenate([jnp.broadcast_to(e4(blk4 * b + p), (16, width)) for b in range(blk16)], axis=0)
            q2.append((q * jnp.exp2(bcb - edge)).astype(BF16))
        terms = [(q * k).astype(BF16)]
        for delta in range(1, 4):
            shift = delta if d == 0 else CHUNK - delta
            ks = pltpu.roll(k, shift, 0)
            bs = pltpu.roll(bcb, shift, 0)
            terms.append((q * ks * jnp.exp2(bcb - bs)).astype(BF16))
        return q1, k1, jnp.concatenate(q2, axis=0), k2, jnp.concatenate(terms, axis=0), others

    def scan_step(i, carry):
        chunks = (i, _backward_chunk(i, n_ctx_chunks, n_chunks))
        rrs = [pl.ds(pl.multiple_of(c * CHUNK, CHUNK), CHUNK) for c in chunks]
        qs = [p_ref[0, rrs[d], 0:width] for d in range(2)]
        vs = [p_ref[0, rrs[d], 3 * width:4 * width] for d in range(2)]
        ks = [kk_s[rrs[d], d * width:(d + 1) * width] for d in range(2)]
        bcs = [bc_s[rrs[d], d * width:(d + 1) * width] for d in range(2)]
        b_last = [bcs[0][CHUNK - 1:CHUNK, :], bcs[1][0:1, :]]
        ops = [score_operands(d, qs[d], ks[d], bcs[d]) for d in range(2)]
        r1 = [_dot_nt(ops[d][0], ops[d][1]) for d in range(2)]
        r2 = [_dot_nt(ops[d][2], ops[d][3]) for d in range(2)]
        r3 = [_dot(ops[d][4], bdones) for d in range(2)]
        states = [st_s[d] for d in range(2)]
        upd = [_dot_tn(vs[d].astype(BF16), (ks[d] * jnp.exp2(b_last[d] - bcs[d])).astype(BF16)) for d in range(2)]
        from_state = [_dot_nt((qs[d] * jnp.exp2(bcs[d])).astype(BF16), states[d].astype(BF16) * bdones)
                      for d in range(2)]
        for d in range(2):
            st_s[d] = states[d] * jnp.exp2(b_last[d]) + upd[d]
        outs = []
        for d in range(2):
            scores = jnp.zeros((CHUNK, width), F32)
            for n, j in enumerate(ops[d][5]):
                scores = jnp.where(codes[d][0] == j, r1[d][n * CHUNK:(n + 1) * CHUNK], scores)
                scores = jnp.where(codes[d][1] == j, r2[d][n * CHUNK:(n + 1) * CHUNK], scores)
            for delta in range(4):
                scores = jnp.where(codes[d][2] == delta, r3[d][delta * CHUNK:(delta + 1) * CHUNK], scores)
            outs.append(from_state[d] + _dot(scores.astype(BF16), _blockdiag(vs[d], bdones)))
        for d in range(2):
            os_s[rrs[d], :] = os_s[rrs[d], :] + outs[d]
        return carry

    lax.fori_loop(0, n_chunks, scan_step, 0)

    def out_tile(t, carry):
        r0 = pl.multiple_of(t * 256, 256)
        o = os_s[pl.ds(r0, 256), :]
        ms = _dot((o * o).astype(BF16), bdones) * (1.0 / HEAD_DIM)
        gate = p_ref[0, pl.ds(r0, 256), 4 * width:5 * width]
        y = o * lax.rsqrt(ms + EPS) * ng_ref[...] * (gate * _sigmoid(gate))
        o_ref[0, pl.ds(r0, 256), :] = y.astype(o_ref.dtype)
        return carry

    lax.fori_loop(0, s_len // 256, out_tile, 0)


def _hg_call(p_hg, lb_l, norm_g, ctx_len):
    b, s, n = p_hg.shape
    width = GROUP_LANES
    ng = jnp.tile(norm_g.astype(F32), width // HEAD_DIM).reshape(1, width)
    const = lambda shape: pl.BlockSpec(shape, lambda i: (0, 0))
    return pl.pallas_call(
        functools.partial(_hg_kernel, ctx_len=ctx_len),
        out_shape=jax.ShapeDtypeStruct((b, s, width), BF16),
        grid=(b,),
        in_specs=[pl.BlockSpec((1, s, n), lambda i: (i, 0, 0)), const((2, width)), const((1, width))],
        out_specs=pl.BlockSpec((1, s, width), lambda i: (i, 0, 0)),
        scratch_shapes=[pltpu.VMEM((s, 2 * width), F32), pltpu.VMEM((s, 2 * width), F32),
                        pltpu.VMEM((s, width), F32), pltpu.VMEM((2, width, width), F32)],
        compiler_params=pltpu.CompilerParams(dimension_semantics=("arbitrary",), vmem_limit_bytes=VMEM_LIMIT),
        name="hgrn2",
    )(p_hg, lb_l, ng)


def _swa_kernel(sink_ref, p_ref, cos_ref, sin_ref, o_ref, q_s, k_s, v_s, bias_s, *, ctx_len):
    s_len = p_ref.shape[1]
    t_len = s_len - ctx_len
    blk = SWA_BLOCK
    qw = SWA_HEADS * HEAD_DIM
    kvw = SWA_KV_HEADS * HEAD_DIM
    group = SWA_HEADS // SWA_KV_HEADS
    gw = group * HEAD_DIM
    scale = HEAD_DIM ** -0.5
    half = HEAD_DIM // 2
    lane128 = _iota((blk, kvw), 1)
    first_half_q = (_iota((blk, qw), 1) & (HEAD_DIM - 1)) < half
    first_half_k = (lane128 & (HEAD_DIM - 1)) < half
    low_head = lane128 < HEAD_DIM

    def kv_tiles(x):
        sw = pltpu.roll(x, HEAD_DIM, 1)
        return jnp.where(low_head, x, sw), jnp.where(low_head, sw, x)

    def prep_tile(t, carry):
        r0 = pl.multiple_of(t * blk, blk)
        q = p_ref[0, pl.ds(r0, blk), 0:qw]
        k = p_ref[0, pl.ds(r0, blk), qw:qw + kvw]
        v = p_ref[0, pl.ds(r0, blk), qw + kvw:qw + 2 * kvw]
        pos = pl.multiple_of(jnp.maximum(r0 - ctx_len, 0), blk)
        cs = cos_ref[pl.ds(pos, blk), :]
        sn = sin_ref[pl.ds(pos, blk), :]
        is_ctx = r0 < ctx_len
        cs = jnp.where(is_ctx, 1.0, cs)
        sn = jnp.where(is_ctx, 0.0, sn)
        q_sw = jnp.where(first_half_q, pltpu.roll(q, qw - half, 1), pltpu.roll(q, half, 1))
        k_sw = jnp.where(first_half_k, pltpu.roll(k, kvw - half, 1), pltpu.roll(k, half, 1))
        cs4 = jnp.concatenate([cs] * (qw // kvw), axis=1)
        sn4 = jnp.concatenate([sn] * (qw // kvw), axis=1)
        q_s[pl.ds(r0, blk), :] = ((q * cs4 + q_sw * sn4) * (scale * LOG2E)).astype(BF16)
        k0, k1 = kv_tiles(k * cs + k_sw * sn)
        v0, v1 = kv_tiles(v)
        k_s[0, pl.ds(r0, blk), :] = k0.astype(BF16)
        k_s[1, pl.ds(r0, blk), :] = k1.astype(BF16)
        v_s[0, pl.ds(r0, blk), :] = v0.astype(BF16)
        v_s[1, pl.ds(r0, blk), :] = v1.astype(BF16)
        return carry

    lax.fori_loop(0, s_len // blk, prep_tile, 0)

    rows = group * blk
    blk_shift = blk.bit_length() - 1
    qmask = _mask_bf16((_iota((rows, gw), 0) >> blk_shift) == (_iota((rows, gw), 1) >> 6))
    rb1 = _iota((rows, 1), 0) >> blk_shift
    lane_g = _iota((blk, gw), 1) >> 6
    sink = []
    for g in range(SWA_KV_HEADS):
        sk = jnp.zeros((rows, 1), F32)
        for j in range(group):
            sk = jnp.where(rb1 == j, sink_ref[g * group + j] * LOG2E, sk)
        sink.append(sk)

    rel = _iota((blk, 3 * blk), 0) - _iota((blk, 3 * blk), 1)
    for case in range(3):
        bias_s[case] = jnp.where(jnp.abs(rel + case * blk) <= WINDOW, 0.0, NEG_INF)

    def wide(x):
        return jnp.concatenate([x, x], axis=1)

    def attend(r0, local_start):
        gs = range(SWA_KV_HEADS)
        qst = []
        for g in gs:
            qb = q_s[pl.ds(r0, blk), g * gw:(g + 1) * gw]
            qst.append(jnp.concatenate([qb] * group, axis=0) * qmask)
        s_ctx = [_dot_nt(qst[g], wide(k_s[g, 0:ctx_len, :])) for g in gs]
        m = [jnp.maximum(jnp.max(s_ctx[g], axis=-1, keepdims=True), sink[g]) for g in gs]
        if local_start is not None:
            bias = bias_s[(r0 - ctx_len - local_start) >> blk_shift]
            bias = jnp.concatenate([bias] * group, axis=0)
            loc = pl.ds(ctx_len + local_start, 3 * blk)
            s_loc = [_dot_nt(qst[g], wide(k_s[g, loc, :])) + bias for g in gs]
            m = [jnp.maximum(m[g], jnp.max(s_loc[g], axis=-1, keepdims=True)) for g in gs]
            e_loc = [jnp.exp2(s_loc[g] - m[g]) for g in gs]
        e_ctx = [jnp.exp2(s_ctx[g] - m[g]) for g in gs]
        den = [jnp.sum(e_ctx[g], axis=-1, keepdims=True) + jnp.exp2(sink[g] - m[g]) for g in gs]
        acc = [_dot(e_ctx[g].astype(BF16), wide(v_s[g, 0:ctx_len, :])) for g in gs]
        if local_start is not None:
            den = [den[g] + jnp.sum(e_loc[g], axis=-1, keepdims=True) for g in gs]
            acc = [acc[g] + _dot(e_loc[g].astype(BF16), wide(v_s[g, loc, :])) for g in gs]
        for g in gs:
            a = acc[g] / den[g]
            out = jnp.zeros((blk, gw), F32)
            for j in range(group):
                out = jnp.where(lane_g == j, a[j * blk:(j + 1) * blk], out)
            o_ref[0, pl.ds(r0, blk), g * gw:(g + 1) * gw] = out.astype(o_ref.dtype)

    def ctx_block(t, carry):
        attend(pl.multiple_of(t * blk, blk), None)
        return carry

    lax.fori_loop(0, ctx_len // blk, ctx_block, 0)

    def lat_block(t, carry):
        r0 = pl.multiple_of(ctx_len + t * blk, blk)
        attend(r0, pl.multiple_of(jnp.clip((t - 1) * blk, 0, t_len - 3 * blk), blk))
        return carry

    lax.fori_loop(0, t_len // blk, lat_block, 0)


def _swa_call(p_sw, sink, cos_t, sin_t, ctx_len):
    b, s, n = p_sw.shape
    qw = SWA_HEADS * HEAD_DIM
    kvw = SWA_KV_HEADS * HEAD_DIM
    t_len = s - ctx_len
    return pl.pallas_call(
        functools.partial(_swa_kernel, ctx_len=ctx_len),
        out_shape=jax.ShapeDtypeStruct((b, s, qw), BF16),
        grid=(b,),
        in_specs=[pl.BlockSpec(memory_space=pltpu.SMEM),
                  pl.BlockSpec((1, s, n), lambda i: (i, 0, 0)),
                  pl.BlockSpec((t_len, kvw), lambda i: (0, 0)),
                  pl.BlockSpec((t_len, kvw), lambda i: (0, 0))],
        out_specs=pl.BlockSpec((1, s, qw), lambda i: (i, 0, 0)),
        scratch_shapes=[pltpu.VMEM((s, qw), BF16), pltpu.VMEM((SWA_KV_HEADS, s, kvw), BF16),
                        pltpu.VMEM((SWA_KV_HEADS, s, kvw), BF16), pltpu.VMEM((3, SWA_BLOCK, 3 * SWA_BLOCK), F32)],
        compiler_params=pltpu.CompilerParams(dimension_semantics=("arbitrary",), vmem_limit_bytes=VMEM_LIMIT),
        name="swa",
    )(sink.astype(F32), p_sw, cos_t, sin_t)


def _rope_tables(t_len):
    rows = t_len // GRID_W
    row = jnp.repeat(jnp.arange(rows), GRID_W).astype(F32)
    col = jnp.tile(jnp.arange(GRID_W), rows).astype(F32)
    half = HEAD_DIM // 2
    inv = ROPE_BASE ** (-jnp.arange(0, half, 2, dtype=F32) / half)
    ang = jnp.concatenate([row[:, None] * inv, col[:, None] * inv], axis=-1)
    cos, sin = jnp.cos(ang), jnp.sin(ang)
    return jnp.tile(cos, (1, 4)), jnp.tile(jnp.concatenate([-sin, sin], axis=-1), (1, 2))


def kernel(x, c, ctx, c_ctx, w_ada, b_ada, norm1, norm2, w_in, dn_conv, dn_A_log, dn_dt_bias, dn_norm, swa_sink,
           hg_lb_logits, hg_norm, w_out, w_ff1, w_ff2, norm_f):
    b, t_len, d = x.shape
    ctx_len = ctx.shape[1]
    depth = w_ada.shape[0]
    assert b + 1 <= MOD_ROWS and ctx_len % MOD_SUB == 0 and t_len % MOD_SUB == 0 and t_len >= 3 * SWA_BLOCK
    tm = max(n * MOD_SUB for n in (1, 2, 3) if (ctx_len + t_len) % (n * MOD_SUB) == 0)
    tm_last = max(n * MOD_SUB for n in (1, 2, 3) if t_len % (n * MOD_SUB) == 0 and ctx_len % (n * MOD_SUB) == 0)

    cc = jnp.zeros((MOD_ROWS, d), F32).at[:b].set(c.astype(F32)).at[b].set(c_ctx.astype(F32))
    mods = _ada_call(cc, w_ada.astype(F32), b_ada.astype(F32))
    lb_all = _lb_call(hg_lb_logits)
    cos_t, sin_t = _rope_tables(t_len)

    n_head = 4 * GROUP_LANES + 16
    w_in_p = jnp.concatenate(
        [w_in[:, :, :n_head], jnp.zeros((depth, d, 128 - 16), w_in.dtype), w_in[:, :, n_head:]], axis=-1).astype(BF16)
    w_out_b = w_out.astype(BF16)
    w1_b = w_ff1.astype(BF16)
    w2_b = w_ff2.astype(BF16)

    xs = jnp.concatenate([ctx.astype(F32), x.astype(F32)], axis=1)
    for l in range(depth):
        last = l == depth - 1
        p_dn, p_sw, p_hg = _in_proj_call(xs, mods, norm1[l].astype(F32), w_in_p, l, ctx_len, tm)
        y_dn = _dn_call(p_dn, dn_conv[l], dn_A_log[l], dn_dt_bias[l], dn_norm[l], ctx_len)
        y_sw = _swa_call(p_sw, swa_sink[l], cos_t, sin_t, ctx_len)
        y_hg = _hg_call(p_hg, lb_all[:, l], hg_norm[l], ctx_len)
        xs = _out_mlp_call(xs, y_dn, y_sw, y_hg, mods, norm2[l].astype(F32), norm_f.astype(F32),
                           w_out_b, w1_b, w2_b, l, ctx_len, tm_last if last else tm, latent_only=last)
    return xs.astype(x.dtype)
```

```python
import functools
import math

import jax
import jax.numpy as jnp
from jax import lax
from jax.experimental import pallas as pl
from jax.experimental.pallas import tpu as pltpu

F32 = jnp.float32
BF16 = jnp.bfloat16

HEAD_DIM = 64
GROUP_LANES = 256
CHUNK = 64
N_MOD = 6
EPS = 1e-6
CONV_W = 5
SWA_HEADS = 8
SWA_KV_HEADS = 2
SWA_BLOCK = 128
WINDOW = 128
GRID_W = 64
ROPE_BASE = 10000.0
MOD_ROWS = 24
MOD_SUB = 256
VMEM_LIMIT = 56 * 1024 * 1024
NEG_INF = float("-inf")
LOG2E = 1.0 / math.log(2.0)


def _dot(a, b):
    return jnp.dot(a, b, preferred_element_type=F32)


def _dot_nt(a, b):
    return lax.dot_general(a, b, (((1,), (1,)), ((), ())), preferred_element_type=F32)


def _dot_tn(a, b):
    return lax.dot_general(a, b, (((0,), (0,)), ((), ())), preferred_element_type=F32)


def _split(x):
    hi = x.astype(BF16)
    lo = (x - hi.astype(F32)).astype(BF16)
    return hi, lo


def _iota(shape, dim):
    return lax.broadcasted_iota(jnp.int32, shape, dim)


def _sigmoid(x):
    return 1.0 / (1.0 + jnp.exp(-x))


def _softplus(x):
    return jnp.maximum(x, 0.0) + jnp.log1p(jnp.exp(-jnp.abs(x)))


def _head_block_mask(n):
    return (_iota((n, n), 0) >> 6) == (_iota((n, n), 1) >> 6)


def _mask_bf16(cond):
    return jnp.where(cond, 1.0, 0.0).astype(BF16)


def _blockdiag(x, ones_bd):
    xb = x.astype(BF16)
    return jnp.concatenate([xb, xb, xb, xb], axis=0) * ones_bd


def _chunk_cumsum_mats():
    r = _iota((256, 512), 0)
    c = _iota((256, 512), 1) & 255
    same = (r >> 6) == (c >> 6)
    return _mask_bf16(same & (c <= r)), _mask_bf16(same & (c >= r))


def _chunk_cumsums(x, low, upp):
    hi, lo = _split(x)
    c0 = _dot(low, jnp.concatenate([hi[:, :256], lo[:, :256]], axis=0))
    c1 = _dot(upp, jnp.concatenate([hi[:, 256:], lo[:, 256:]], axis=0))
    return c0, c1


def _backward_chunk(i, n_ctx_chunks, n_chunks):
    return jnp.where(i < n_ctx_chunks, n_ctx_chunks - 1 - i, n_chunks - 1 + n_ctx_chunks - i)


def _ada_kernel(c_ref, w_ref, b_ref, o_ref):
    c = c_ref[...]
    a = c * _sigmoid(c)
    a_hi, a_lo = _split(a)
    w = w_ref[0]
    w_hi, w_lo = _split(w)
    acc = _dot(a_hi, w_hi) + _dot(a_lo, w_hi) + _dot(a_hi, w_lo)
    o_ref[0] = acc + b_ref[0]


def _ada_call(cc, w_ada, b_ada):
    depth, d, n = w_ada.shape
    tn = 1536
    return pl.pallas_call(
        _ada_kernel,
        out_shape=jax.ShapeDtypeStruct((depth, MOD_ROWS, n), F32),
        grid=(depth, n // tn),
        in_specs=[
            pl.BlockSpec((MOD_ROWS, d), lambda l, j: (0, 0)),
            pl.BlockSpec((1, d, tn), lambda l, j: (l, 0, j)),
            pl.BlockSpec((1, 1, tn), lambda l, j: (l, 0, j)),
        ],
        out_specs=pl.BlockSpec((1, MOD_ROWS, tn), lambda l, j: (l, 0, j)),
        compiler_params=pltpu.CompilerParams(
            dimension_semantics=("arbitrary", "arbitrary"), vmem_limit_bytes=VMEM_LIMIT),
        name="ada_mod",
    )(cc, w_ada, b_ada.reshape(depth, 1, n))


def _lb_kernel(x_ref, o_ref):
    depth = x_ref.shape[1]
    rows = [x_ref[:, l, :] for l in range(depth)]
    m = rows[0]
    for r in rows[1:]:
        m = jnp.maximum(m, r)
    es = [jnp.exp(r - m) for r in rows]
    tot = es[0]
    for e in es[1:]:
        tot = tot + e
    ps = [e / tot for e in es]
    run = ps[0]
    o_ref[:, 0, :] = run - ps[0]
    for l in range(1, depth):
        run = run + ps[l]
        o_ref[:, l, :] = run - ps[0]


def _lb_call(logits):
    return pl.pallas_call(
        _lb_kernel,
        out_shape=jax.ShapeDtypeStruct(logits.shape, F32),
        name="hg_lower_bounds",
    )(logits.astype(F32))


def _mod_row(mods_ref, row, idx, d):
    return mods_ref[pl.ds(row, 1), idx * d:(idx + 1) * d]


def _rms(x):
    return x * lax.rsqrt(jnp.mean(x * x, axis=-1, keepdims=True) + EPS)


def _mod_rows(ctx_subs, n_batch, tile_offset, n_sub):
    first = (pl.program_id(1) + tile_offset) * n_sub
    return [jnp.where(first + i < ctx_subs, n_batch, pl.program_id(0)) for i in range(n_sub)]


def _in_proj_kernel(x_ref, mods_ref, g_ref, w_ref, dn_ref, sw_ref, hg_ref, *, ctx_subs, n_batch):
    d = x_ref.shape[-1]
    n_sub = x_ref.shape[1] // MOD_SUB
    hs = []
    for i, row in enumerate(_mod_rows(ctx_subs, n_batch, 0, n_sub)):
        x = x_ref[0, i * MOD_SUB:(i + 1) * MOD_SUB, :]
        h = _rms(x) * g_ref[...] * (1.0 + _mod_row(mods_ref, row, 1, d)) + _mod_row(mods_ref, row, 0, d)
        hs.append(h.astype(BF16))
    p = _dot(jnp.concatenate(hs, axis=0), w_ref[...])
    n_dn = dn_ref.shape[-1]
    n_sw = sw_ref.shape[-1]
    dn_ref[0] = p[:, :n_dn]
    sw_ref[0] = p[:, n_dn:n_dn + n_sw]
    hg_ref[0] = p[:, n_dn + n_sw:]


def _in_proj_call(x, mods, g1, w_in_p, layer, ctx_len, tm):
    b, s, d = x.shape
    n_dn, n_sw, n_hg = 1152, 768, 1280
    n_all = w_in_p.shape[-1]
    kern = functools.partial(_in_proj_kernel, ctx_subs=ctx_len // MOD_SUB, n_batch=b)
    return pl.pallas_call(
        kern,
        out_shape=(jax.ShapeDtypeStruct((b, s, n_dn), F32),
                   jax.ShapeDtypeStruct((b, s, n_sw), F32),
                   jax.ShapeDtypeStruct((b, s, n_hg), F32)),
        grid=(b, s // tm),
        in_specs=[
            pl.BlockSpec((1, tm, d), lambda i, t: (i, t, 0)),
            pl.BlockSpec((None,) + mods.shape[1:], lambda i, t: (layer, 0, 0)),
            pl.BlockSpec((1, d), lambda i, t: (0, 0)),
            pl.BlockSpec((None, d, n_all), lambda i, t: (layer, 0, 0), pipeline_mode=pl.Buffered(1)),
        ],
        out_specs=(pl.BlockSpec((1, tm, n_dn), lambda i, t: (i, t, 0)),
                   pl.BlockSpec((1, tm, n_sw), lambda i, t: (i, t, 0)),
                   pl.BlockSpec((1, tm, n_hg), lambda i, t: (i, t, 0))),
        compiler_params=pltpu.CompilerParams(
            dimension_semantics=("arbitrary", "arbitrary"), vmem_limit_bytes=VMEM_LIMIT),
        name="in_proj",
    )(x, mods, g1.reshape(1, d), w_in_p)


def _out_mlp_kernel(x_ref, dn_ref, sw_ref, hg_ref, mods_ref, g2_ref, gf_ref, wo_ref, w1_ref, w2_ref, o_ref,
                    *, ctx_subs, tile_offset, n_batch, final_norm):
    d = x_ref.shape[-1]
    n_sub = x_ref.shape[1] // MOD_SUB
    rows = _mod_rows(ctx_subs, n_batch, tile_offset, n_sub)
    subs = [slice(i * MOD_SUB, (i + 1) * MOD_SUB) for i in range(n_sub)]
    y = jnp.concatenate([dn_ref[0], sw_ref[0], hg_ref[0]], axis=-1)
    mix = _dot(y, wo_ref[...])
    x1 = [x_ref[0, s, :] + _mod_row(mods_ref, r, 2, d) * mix[s] for r, s in zip(rows, subs)]
    hb = jnp.concatenate(
        [(_rms(xs) * g2_ref[...] * (1.0 + _mod_row(mods_ref, r, 4, d)) + _mod_row(mods_ref, r, 3, d)).astype(BF16)
         for r, xs in zip(rows, x1)], axis=0)
    d_ff = w1_ref.shape[1]
    ff_chunk = 1024
    acc = jnp.zeros(mix.shape, F32)
    for j in range(d_ff // ff_chunk):
        a = _dot(hb, w1_ref[:, j * ff_chunk:(j + 1) * ff_chunk])
        a = jnp.square(jnp.maximum(a, 0.0))
        acc = acc + _dot(a.astype(BF16), w2_ref[j * ff_chunk:(j + 1) * ff_chunk, :])
    for r, s, xs in zip(rows, subs, x1):
        x2 = xs + _mod_row(mods_ref, r, 5, d) * acc[s]
        if final_norm:
            x2 = _rms(x2) * gf_ref[...]
        o_ref[0, s, :] = x2


def _out_mlp_call(x, y_dn, y_sw, y_hg, mods, g2, gf, w_out, w1, w2, layer, ctx_len, tm, latent_only):
    b, s, d = x.shape
    d_ff = w1.shape[-1]
    off = ctx_len // tm if latent_only else 0
    s_out = s - ctx_len if latent_only else s
    kern = functools.partial(_out_mlp_kernel, ctx_subs=ctx_len // MOD_SUB, tile_offset=off, n_batch=b,
                             final_norm=latent_only)
    tok = lambda w: pl.BlockSpec((1, tm, w), lambda i, t: (i, t + off, 0))
    const = lambda shape: pl.BlockSpec(shape, lambda i, t: (0, 0))
    per_layer = lambda shape, **kw: pl.BlockSpec((None,) + shape, lambda i, t: (layer, 0, 0), **kw)
    weight = lambda shape: per_layer(shape, pipeline_mode=pl.Buffered(1))
    return pl.pallas_call(
        kern,
        out_shape=jax.ShapeDtypeStruct((b, s_out, d), F32),
        grid=(b, s_out // tm),
        in_specs=[tok(d), tok(y_dn.shape[-1]), tok(y_sw.shape[-1]), tok(y_hg.shape[-1]),
                  per_layer(mods.shape[1:]), const((1, d)), const((1, d)),
                  weight((d, d)), weight((d, d_ff)), weight((d_ff, d))],
        out_specs=pl.BlockSpec((1, tm, d), lambda i, t: (i, t, 0)),
        compiler_params=pltpu.CompilerParams(
            dimension_semantics=("arbitrary", "arbitrary"), vmem_limit_bytes=VMEM_LIMIT),
        name="out_mlp",
    )(x, y_dn, y_sw, y_hg, mods, g2.reshape(1, d), gf.reshape(1, d), w_out, w1, w2)


def _dn_kernel(p_ref, cw_ref, alog_ref, dtb_ref, ng_ref, o_ref, xs_s, u_s, w_s, sc_s, qg_s, kd_s, eg_s, os_s, st_s,
               *, ctx_len):
    s_len = p_ref.shape[1]
    n_chunks = s_len // CHUNK
    n_ctx_chunks = ctx_len // CHUNK
    width = GROUP_LANES
    sub = 4 * CHUNK
    tile = sub
    bdmask = _head_block_mask(width)
    bdones = _mask_bf16(bdmask)
    low, upp = _chunk_cumsum_mats()
    expand = _mask_bf16((_iota((256, 1024), 1) >> 6) == (_iota((256, 1024), 0) & 127))
    row = _iota((CHUNK, width), 0)
    col = _iota((CHUNK, width), 1) & 63
    eye = jnp.where(row == col, 1.0, 0.0)

    def solve_chunks(chains):
        n = len(chains)
        a_mats, scs, rhss, tails = [], [], [], []
        for d, q, k, v, bt, gcb in chains:
            if d == 0:
                incl, strict = col <= row, col < row
                gtot = gcb[CHUNK - 1:CHUNK, :]
            else:
                incl, strict = col >= row, col > row
                gtot = gcb[0:1, :]
            grow = jnp.sum(gcb * eye, axis=0, keepdims=True)
            decay = jnp.exp2(jnp.where(incl, gcb - grow, NEG_INF))
            kb = k * bt
            egc = jnp.exp2(gcb)
            kq = jnp.concatenate([kb, q], axis=0).astype(BF16)
            raw = _dot_nt(kq, _blockdiag(k, bdones))
            a_mats.append(jnp.where(strict, raw[0:CHUNK] * decay, 0.0))
            scs.append((raw[CHUNK:2 * CHUNK] * decay).astype(BF16))
            rhss.append(jnp.concatenate([_blockdiag(v * bt, bdones), _blockdiag(kb * egc, bdones)], axis=1))
            tails.append(((q * egc).astype(BF16), (k * jnp.exp2(gtot - gcb)).astype(BF16), jnp.exp2(gtot)))
        ps = [eye - jnp.where((row >> 1) == (col >> 1), a, 0.0) for a in a_mats]
        for lg in range(1, 6):
            pair = ((row >> (lg + 1)) == (col >> (lg + 1))) & ((row >> lg) != (col >> lg))
            cts = [_dot(jnp.where(pair, a_mats[i], 0.0).astype(BF16), _blockdiag(ps[i], bdones)) for i in range(n)]
            ps = [ps[i] - _dot(ps[i].astype(BF16), _blockdiag(cts[i], bdones)) for i in range(n)]
        outs = []
        for i in range(n):
            uw = _dot(ps[i].astype(BF16), rhss[i])
            outs.append((uw[:, 0:width], uw[:, width:2 * width].astype(BF16), scs[i]) + tails[i])
        return outs

    gap = 8
    n_slabs = 3 * width // 128
    zeros_gap = jnp.zeros((n_slabs, gap, 128), F32)
    xs_s[:, 0:gap, :] = zeros_gap
    xs_s[:, gap + ctx_len:2 * gap + ctx_len, :] = zeros_gap
    xs_s[:, 2 * gap + s_len:3 * gap + s_len, :] = zeros_gap

    def padded_row(r0):
        return r0 + gap + jnp.where(r0 >= ctx_len, gap, 0)

    def copy_tile(t, carry):
        r0 = pl.multiple_of(t * sub, sub)
        dst = pl.ds(pl.multiple_of(padded_row(r0), 8), sub)
        for c in range(n_slabs):
            xs_s[c, dst, :] = p_ref[0, pl.ds(r0, sub), c * 128:(c + 1) * 128]
        return carry

    lax.fori_loop(0, s_len // sub, copy_tile, 0)
    lane_ab = _iota((sub, 128), 1)

    def conv_tap(base, j):
        rows = pl.ds(base + j, sub)
        return jnp.concatenate([xs_s[c, rows, :] for c in range(n_slabs)], axis=1) * cw_ref[j:j + 1, :]

    def sub_tile_chains(r0):
        base = padded_row(r0) - CONV_W // 2
        acc = conv_tap(base, 0)
        for j in range(1, CONV_W):
            acc = acc + conv_tap(base, j)
        y = acc * _sigmoid(acc)
        q = y[:, 0:width]
        k = y[:, width:2 * width]
        v = y[:, 2 * width:3 * width]
        q = q * lax.rsqrt(_dot((q * q).astype(BF16), bdones) + EPS) * (HEAD_DIM ** -0.5)
        k = k * lax.rsqrt(_dot((k * k).astype(BF16), bdones) + EPS)
        ab = p_ref[0, pl.ds(r0, sub), 4 * width:4 * width + 128]
        narrow = jnp.where(lane_ab < 8, (-LOG2E) * jnp.exp(alog_ref[...]) * _softplus(ab + dtb_ref[...]),
                           _sigmoid(ab))
        hi, lo = _split(narrow)
        abx = _dot(jnp.concatenate([hi, lo], axis=1), expand)
        cums = _chunk_cumsums(abx[:, 0:2 * width], low, upp)
        beta = abx[:, 2 * width:4 * width]
        chains = []
        for cc in range(sub // CHUNK):
            rs = slice(cc * CHUNK, (cc + 1) * CHUNK)
            for d in range(2):
                chains.append((d, q[rs], k[rs], v[rs], beta[rs, d * width:(d + 1) * width], cums[d][rs]))
        return chains

    def prep_tile(t, carry):
        r0 = pl.multiple_of(t * tile, tile)
        os_s[pl.ds(r0, tile), :] = jnp.zeros((tile, width), F32)
        chains = []
        for sb in range(tile // sub):
            chains += sub_tile_chains(pl.multiple_of(r0 + sb * sub, sub))
        for idx, (u, w, sc, qg, kd, eg) in enumerate(solve_chunks(chains)):
            cc, d = idx // 2, idx % 2
            rr = pl.ds(r0 + cc * CHUNK, CHUNK)
            er = pl.ds(pl.multiple_of((t * (tile // CHUNK) + cc) * 8, 8), 8)
            u_s[d, rr, :] = u
            w_s[d, rr, :] = w
            sc_s[d, rr, :] = sc
            qg_s[d, rr, :] = qg
            kd_s[d, rr, :] = kd
            eg_s[d, er, :] = jnp.broadcast_to(eg, (8, width))
        return carry

    lax.fori_loop(0, s_len // tile, prep_tile, 0)

    st_s[...] = jnp.zeros(st_s.shape, F32)

    def scan_step(i, carry):
        chunks = (i, _backward_chunk(i, n_ctx_chunks, n_chunks))
        rrs = [pl.ds(pl.multiple_of(c * CHUNK, CHUNK), CHUNK) for c in chunks]
        states = [st_s[d] for d in range(2)]
        ws_qs = [_dot(jnp.concatenate([w_s[d, rrs[d], :], qg_s[d, rrs[d], :]], axis=0),
                      states[d].astype(BF16) * bdones) for d in range(2)]
        v_new = [u_s[d, rrs[d], :] - ws_qs[d][0:CHUNK] for d in range(2)]
        upd = [_dot_tn(kd_s[d, rrs[d], :], v_new[d].astype(BF16)) for d in range(2)]
        for d in range(2):
            eg = eg_s[d, pl.ds(pl.multiple_of(chunks[d] * 8, 8), 1), :]
            st_s[d] = states[d] * eg + upd[d]
        outs = [ws_qs[d][CHUNK:2 * CHUNK] + _dot(sc_s[d, rrs[d], :], _blockdiag(v_new[d], bdones)) for d in range(2)]
        for d in range(2):
            os_s[rrs[d], :] = os_s[rrs[d], :] + outs[d]
        return carry

    lax.fori_loop(0, n_chunks, scan_step, 0)

    def out_tile(t, carry):
        r0 = pl.multiple_of(t * 256, 256)
        o = os_s[pl.ds(r0, 256), :]
        ms = _dot((o * o).astype(BF16), bdones) * (1.0 / HEAD_DIM)
        gate = p_ref[0, pl.ds(r0, 256), 3 * width:4 * width]
        y = o * lax.rsqrt(ms + EPS) * ng_ref[...] * (gate * _sigmoid(gate))
        o_ref[0, pl.ds(r0, 256), :] = y.astype(o_ref.dtype)
        return carry

    lax.fori_loop(0, s_len // 256, out_tile, 0)


def _dn_call(p_dn, conv_w, a_log, dt_bias, norm_g, ctx_len):
    b, s, n = p_dn.shape
    width = GROUP_LANES
    lanes = lambda a: jnp.zeros((1, 128), F32).at[0, :a.size].set(a.astype(F32).reshape(-1))
    ng = jnp.tile(norm_g.astype(F32), width // HEAD_DIM).reshape(1, width)
    const = lambda shape: pl.BlockSpec(shape, lambda i: (0, 0))
    return pl.pallas_call(
        functools.partial(_dn_kernel, ctx_len=ctx_len),
        out_shape=jax.ShapeDtypeStruct((b, s, width), BF16),
        grid=(b,),
        in_specs=[pl.BlockSpec((1, s, n), lambda i: (i, 0, 0)),
                  const((CONV_W, 3 * width)), const((1, 128)), const((1, 128)), const((1, width))],
        out_specs=pl.BlockSpec((1, s, width), lambda i: (i, 0, 0)),
        scratch_shapes=[pltpu.VMEM((3 * width // 128, s + 24, 128), F32),
                        pltpu.VMEM((2, s, width), F32)] + [pltpu.VMEM((2, s, width), BF16)] * 4 + [
            pltpu.VMEM((2, s // CHUNK * 8, width), F32), pltpu.VMEM((s, width), F32),
            pltpu.VMEM((2, width, width), F32)],
        compiler_params=pltpu.CompilerParams(dimension_semantics=("arbitrary",), vmem_limit_bytes=VMEM_LIMIT),
        name="deltanet",
    )(p_dn, conv_w.astype(F32), lanes(a_log), lanes(dt_bias), ng)


def _hg_kernel(p_ref, lb_ref, ng_ref, o_ref, bc_s, kk_s, os_s, st_s, *, ctx_len):
    s_len = p_ref.shape[1]
    n_chunks = s_len // CHUNK
    n_ctx_chunks = ctx_len // CHUNK
    width = GROUP_LANES
    bdmask = _head_block_mask(width)
    bdones = _mask_bf16(bdmask)
    low, upp = _chunk_cumsum_mats()
    lb = jnp.concatenate([lb_ref[0:1, :], lb_ref[1:2, :]], axis=1)
    log_lb = jnp.log(lb)
    log_1m = jnp.log1p(-lb)

    def gate_tile(t, carry):
        r0 = pl.multiple_of(t * 256, 256)
        z = p_ref[0, pl.ds(r0, 256), width:3 * width]
        ez = jnp.exp(-jnp.abs(z))
        one_p = 1.0 + ez
        tt = log_1m + (jnp.minimum(z, 0.0) - jnp.log(one_p))
        logf = jnp.maximum(log_lb, tt) + jnp.log(1.0 + jnp.exp(-jnp.abs(log_lb - tt)))
        kk_s[pl.ds(r0, 256), :] = (1.0 - lb) * jnp.where(z > 0.0, ez, 1.0) / one_p
        c0, c1 = _chunk_cumsums(logf * LOG2E, low, upp)
        bc_s[pl.ds(r0, 256), 0:width] = c0
        bc_s[pl.ds(r0, 256), width:2 * width] = c1
        os_s[pl.ds(r0, 256), :] = jnp.zeros((256, width), F32)
        return carry

    lax.fori_loop(0, s_len // 256, gate_tile, 0)

    st_s[...] = jnp.zeros(st_s.shape, F32)
    row = _iota((CHUNK, width), 0)
    col = _iota((CHUNK, width), 1) & 63
    sub8 = _iota((8, width), 0)
    blk16, blk4 = CHUNK // 16, 16 // 4

    codes = []
    for d in range(2):
        ahead16 = ((row >> 4) > (col >> 4)) if d == 0 else ((row >> 4) < (col >> 4))
        rp, cp = (row >> 2) & 3, (col >> 2) & 3
        ahead4 = ((row >> 4) == (col >> 4)) & ((rp > cp) if d == 0 else (rp < cp))
        delta = (row - col) if d == 0 else (col - row)
        codes.append((jnp.where(ahead16, col >> 4, -1), jnp.where(ahead4, cp, -1),
                      jnp.where(((row >> 2) == (col >> 2)) & (delta >= 0), delta, -1)))

    def score_operands(d, q, k, bcb):
        if d == 0:
            e16 = lambda j: bcb[16 * j + 15:16 * j + 16, :]
            e4 = lambda m: bcb[4 * m + 3:4 * m + 4, :]
            others = range(0, 3)
        else:
            e16 = lambda j: bcb[16 * j:16 * j + 1, :]
            e4 = lambda m: bcb[4 * m:4 * m + 1, :]
            others = range(1, 4)
        key_e16 = jnp.concatenate([jnp.broadcast_to(e16(j), (16, width)) for j in range(blk16)], axis=0)
        key_e4 = jnp.concatenate([jnp.where(sub8 < 4, e4(2 * g), e4(2 * g + 1)) for g in range(CHUNK // 8)], axis=0)
        k1 = _blockdiag(k * jnp.exp2(key_e16 - bcb), bdones)
        k2 = _blockdiag(k * jnp.exp2(key_e4 - bcb), bdones)
        q1 = jnp.concatenate([(q * jnp.exp2(bcb - e16(j))).astype(BF16) for j in others], axis=0)
        q2 = []
        for p in others:
            edge = jnp.concatenate([jnp.broadcast_to(e4(blk4 * b + p), (16, width)) for b in range(blk16)], axis=0)
            q2.append((q * jnp.exp2(bcb - edge)).astype(BF16))
        terms = [(q * k).astype(BF16)]
        for delta in range(1, 4):
            shift = delta if d == 0 else CHUNK - delta
            ks = pltpu.roll(k, shift, 0)
            bs = pltpu.roll(bcb, shift, 0)
            terms.append((q * ks * jnp.exp2(bcb - bs)).astype(BF16))
        return q1, k1, jnp.concatenate(q2, axis=0), k2, jnp.concatenate(terms, axis=0), others

    def scan_step(i, carry):
        chunks = (i, _backward_chunk(i, n_ctx_chunks, n_chunks))
        rrs = [pl.ds(pl.multiple_of(c * CHUNK, CHUNK), CHUNK) for c in chunks]
        qs = [p_ref[0, rrs[d], 0:width] for d in range(2)]
        vs = [p_ref[0, rrs[d], 3 * width:4 * width] for d in range(2)]
        ks = [kk_s[rrs[d], d * width:(d + 1) * width] for d in range(2)]
        bcs = [bc_s[rrs[d], d * width:(d + 1) * width] for d in range(2)]
        b_last = [bcs[0][CHUNK - 1:CHUNK, :], bcs[1][0:1, :]]
        ops = [score_operands(d, qs[d], ks[d], bcs[d]) for d in range(2)]
        r1 = [_dot_nt(ops[d][0], ops[d][1]) for d in range(2)]
        r2 = [_dot_nt(ops[d][2], ops[d][3]) for d in range(2)]
        r3 = [_dot(ops[d][4], bdones) for d in range(2)]
        states = [st_s[d] for d in range(2)]
        upd = [_dot_tn(vs[d].astype(BF16), (ks[d] * jnp.exp2(b_last[d] - bcs[d])).astype(BF16)) for d in range(2)]
        from_state = [_dot_nt((qs[d] * jnp.exp2(bcs[d])).astype(BF16), states[d].astype(BF16) * bdones)
                      for d in range(2)]
        for d in range(2):
            st_s[d] = states[d] * jnp.exp2(b_last[d]) + upd[d]
        outs = []
        for d in range(2):
            scores = jnp.zeros((CHUNK, width), F32)
            for n, j in enumerate(ops[d][5]):
                scores = jnp.where(codes[d][0] == j, r1[d][n * CHUNK:(n + 1) * CHUNK], scores)
                scores = jnp.where(codes[d][1] == j, r2[d][n * CHUNK:(n + 1) * CHUNK], scores)
            for delta in range(4):
                scores = jnp.where(codes[d][2] == delta, r3[d][delta * CHUNK:(delta + 1) * CHUNK], scores)
            outs.append(from_state[d] + _dot(scores.astype(BF16), _blockdiag(vs[d], bdones)))
        for d in range(2):
            os_s[rrs[d], :] = os_s[rrs[d], :] + outs[d]
        return carry

    steps_per_iter = 2 if n_chunks % 2 == 0 else 1

    def scan_iter(i, carry):
        for s in range(steps_per_iter):
            scan_step(i * steps_per_iter + s, carry)
        return carry

    lax.fori_loop(0, n_chunks // steps_per_iter, scan_iter, 0)

    def out_tile(t, carry):
        r0 = pl.multiple_of(t * 256, 256)
        o = os_s[pl.ds(r0, 256), :]
        ms = _dot((o * o).astype(BF16), bdones) * (1.0 / HEAD_DIM)
        gate = p_ref[0, pl.ds(r0, 256), 4 * width:5 * width]
        y = o * lax.rsqrt(ms + EPS) * ng_ref[...] * (gate * _sigmoid(gate))
        o_ref[0, pl.ds(r0, 256), :] = y.astype(o_ref.dtype)
        return carry

    lax.fori_loop(0, s_len // 256, out_tile, 0)


def _hg_call(p_hg, lb_l, norm_g, ctx_len):
    b, s, n = p_hg.shape
    width = GROUP_LANES
    ng = jnp.tile(norm_g.astype(F32), width // HEAD_DIM).reshape(1, width)
    const = lambda shape: pl.BlockSpec(shape, lambda i: (0, 0))
    return pl.pallas_call(
        functools.partial(_hg_kernel, ctx_len=ctx_len),
        out_shape=jax.ShapeDtypeStruct((b, s, width), BF16),
        grid=(b,),
        in_specs=[pl.BlockSpec((1, s, n), lambda i: (i, 0, 0)), const((2, width)), const((1, width))],
        out_specs=pl.BlockSpec((1, s, width), lambda i: (i, 0, 0)),
        scratch_shapes=[pltpu.VMEM((s, 2 * width), F32), pltpu.VMEM((s, 2 * width), F32),
                        pltpu.VMEM((s, width), F32), pltpu.VMEM((2, width, width), F32)],
        compiler_params=pltpu.CompilerParams(dimension_semantics=("arbitrary",), vmem_limit_bytes=VMEM_LIMIT),
        name="hgrn2",
    )(p_hg, lb_l, ng)


def _swa_kernel(sink_ref, p_ref, cos_ref, sin_ref, o_ref, q_s, k_s, v_s, bias_s, *, ctx_len):
    s_len = p_ref.shape[1]
    t_len = s_len - ctx_len
    blk = SWA_BLOCK
    qw = SWA_HEADS * HEAD_DIM
    kvw = SWA_KV_HEADS * HEAD_DIM
    group = SWA_HEADS // SWA_KV_HEADS
    gw = group * HEAD_DIM
    scale = HEAD_DIM ** -0.5
    half = HEAD_DIM // 2
    lane128 = _iota((blk, kvw), 1)
    first_half_q = (_iota((blk, qw), 1) & (HEAD_DIM - 1)) < half
    first_half_k = (lane128 & (HEAD_DIM - 1)) < half
    low_head = lane128 < HEAD_DIM

    def kv_tiles(x):
        sw = pltpu.roll(x, HEAD_DIM, 1)
        return jnp.where(low_head, x, sw), jnp.where(low_head, sw, x)

    def prep_tile(t, carry):
        r0 = pl.multiple_of(t * blk, blk)
        q = p_ref[0, pl.ds(r0, blk), 0:qw]
        k = p_ref[0, pl.ds(r0, blk), qw:qw + kvw]
        v = p_ref[0, pl.ds(r0, blk), qw + kvw:qw + 2 * kvw]
        pos = pl.multiple_of(jnp.maximum(r0 - ctx_len, 0), blk)
        cs = cos_ref[pl.ds(pos, blk), :]
        sn = sin_ref[pl.ds(pos, blk), :]
        is_ctx = r0 < ctx_len
        cs = jnp.where(is_ctx, 1.0, cs)
        sn = jnp.where(is_ctx, 0.0, sn)
        q_sw = jnp.where(first_half_q, pltpu.roll(q, qw - half, 1), pltpu.roll(q, half, 1))
        k_sw = jnp.where(first_half_k, pltpu.roll(k, kvw - half, 1), pltpu.roll(k, half, 1))
        cs4 = jnp.concatenate([cs] * (qw // kvw), axis=1)
        sn4 = jnp.concatenate([sn] * (qw // kvw), axis=1)
        q_s[pl.ds(r0, blk), :] = ((q * cs4 + q_sw * sn4) * (scale * LOG2E)).astype(BF16)
        k0, k1 = kv_tiles(k * cs + k_sw * sn)
        v0, v1 = kv_tiles(v)
        k_s[0, pl.ds(r0, blk), :] = k0.astype(BF16)
        k_s[1, pl.ds(r0, blk), :] = k1.astype(BF16)
        v_s[0, pl.ds(r0, blk), :] = v0.astype(BF16)
        v_s[1, pl.ds(r0, blk), :] = v1.astype(BF16)
        return carry

    lax.fori_loop(0, s_len // blk, prep_tile, 0)

    rows = group * blk
    blk_shift = blk.bit_length() - 1
    qmask = _mask_bf16((_iota((rows, gw), 0) >> blk_shift) == (_iota((rows, gw), 1) >> 6))
    rb1 = _iota((rows, 1), 0) >> blk_shift
    lane_g = _iota((blk, gw), 1) >> 6
    sink = []
    for g in range(SWA_KV_HEADS):
        sk = jnp.zeros((rows, 1), F32)
        for j in range(group):
            sk = jnp.where(rb1 == j, sink_ref[g * group + j] * LOG2E, sk)
        sink.append(sk)

    rel = _iota((blk, 3 * blk), 0) - _iota((blk, 3 * blk), 1)
    for case in range(3):
        bias_s[case] = jnp.where(jnp.abs(rel + case * blk) <= WINDOW, 0.0, NEG_INF)

    def wide(x):
        return jnp.concatenate([x, x], axis=1)

    def attend(r0, local_start):
        gs = range(SWA_KV_HEADS)
        qst = []
        for g in gs:
            qb = q_s[pl.ds(r0, blk), g * gw:(g + 1) * gw]
            qst.append(jnp.concatenate([qb] * group, axis=0) * qmask)
        s_ctx = [_dot_nt(qst[g], wide(k_s[g, 0:ctx_len, :])) for g in gs]
        m = [jnp.maximum(jnp.max(s_ctx[g], axis=-1, keepdims=True), sink[g]) for g in gs]
        if local_start is not None:
            bias = bias_s[(r0 - ctx_len - local_start) >> blk_shift]
            bias = jnp.concatenate([bias] * group, axis=0)
            loc = pl.ds(ctx_len + local_start, 3 * blk)
            s_loc = [_dot_nt(qst[g], wide(k_s[g, loc, :])) + bias for g in gs]
            m = [jnp.maximum(m[g], jnp.max(s_loc[g], axis=-1, keepdims=True)) for g in gs]
            e_loc = [jnp.exp2(s_loc[g] - m[g]) for g in gs]
        e_ctx = [jnp.exp2(s_ctx[g] - m[g]) for g in gs]
        den = [jnp.sum(e_ctx[g], axis=-1, keepdims=True) + jnp.exp2(sink[g] - m[g]) for g in gs]
        acc = [_dot(e_ctx[g].astype(BF16), wide(v_s[g, 0:ctx_len, :])) for g in gs]
        if local_start is not None:
            den = [den[g] + jnp.sum(e_loc[g], axis=-1, keepdims=True) for g in gs]
            acc = [acc[g] + _dot(e_loc[g].astype(BF16), wide(v_s[g, loc, :])) for g in gs]
        for g in gs:
            a = acc[g] / den[g]
            out = jnp.zeros((blk, gw), F32)
            for j in range(group):
                out = jnp.where(lane_g == j, a[j * blk:(j + 1) * blk], out)
            o_ref[0, pl.ds(r0, blk), g * gw:(g + 1) * gw] = out.astype(o_ref.dtype)

    def ctx_block(t, carry):
        attend(pl.multiple_of(t * blk, blk), None)
        return carry

    lax.fori_loop(0, ctx_len // blk, ctx_block, 0)

    n_lat = t_len // blk
    blocks_per_iter = 2 if n_lat % 2 == 0 else 1

    def lat_blocks(i, carry):
        for s in range(blocks_per_iter):
            t = i * blocks_per_iter + s
            r0 = pl.multiple_of(ctx_len + t * blk, blk)
            attend(r0, pl.multiple_of(jnp.clip((t - 1) * blk, 0, t_len - 3 * blk), blk))
        return carry

    lax.fori_loop(0, n_lat // blocks_per_iter, lat_blocks, 0)


def _swa_call(p_sw, sink, cos_t, sin_t, ctx_len):
    b, s, n = p_sw.shape
    qw = SWA_HEADS * HEAD_DIM
    kvw = SWA_KV_HEADS * HEAD_DIM
    t_len = s - ctx_len
    return pl.pallas_call(
        functools.partial(_swa_kernel, ctx_len=ctx_len),
        out_shape=jax.ShapeDtypeStruct((b, s, qw), BF16),
        grid=(b,),
        in_specs=[pl.BlockSpec(memory_space=pltpu.SMEM),
                  pl.BlockSpec((1, s, n), lambda i: (i, 0, 0)),
                  pl.BlockSpec((t_len, kvw), lambda i: (0, 0)),
                  pl.BlockSpec((t_len, kvw), lambda i: (0, 0))],
        out_specs=pl.BlockSpec((1, s, qw), lambda i: (i, 0, 0)),
        scratch_shapes=[pltpu.VMEM((s, qw), BF16), pltpu.VMEM((SWA_KV_HEADS, s, kvw), BF16),
                        pltpu.VMEM((SWA_KV_HEADS, s, kvw), BF16), pltpu.VMEM((3, SWA_BLOCK, 3 * SWA_BLOCK), F32)],
        compiler_params=pltpu.CompilerParams(dimension_semantics=("arbitrary",), vmem_limit_bytes=VMEM_LIMIT),
        name="swa",
    )(sink.astype(F32), p_sw, cos_t, sin_t)


def _rope_tables(t_len):
    rows = t_len // GRID_W
    row = jnp.repeat(jnp.arange(rows), GRID_W).astype(F32)
    col = jnp.tile(jnp.arange(GRID_W), rows).astype(F32)
    half = HEAD_DIM // 2
    inv = ROPE_BASE ** (-jnp.arange(0, half, 2, dtype=F32) / half)
    ang = jnp.concatenate([row[:, None] * inv, col[:, None] * inv], axis=-1)
    cos, sin = jnp.cos(ang), jnp.sin(ang)
    return jnp.tile(cos, (1, 4)), jnp.tile(jnp.concatenate([-sin, sin], axis=-1), (1, 2))


def kernel(x, c, ctx, c_ctx, w_ada, b_ada, norm1, norm2, w_in, dn_conv, dn_A_log, dn_dt_bias, dn_norm, swa_sink,
           hg_lb_logits, hg_norm, w_out, w_ff1, w_ff2, norm_f):
    b, t_len, d = x.shape
    ctx_len = ctx.shape[1]
    depth = w_ada.shape[0]
    assert b + 1 <= MOD_ROWS and ctx_len % MOD_SUB == 0 and t_len % MOD_SUB == 0 and t_len >= 3 * SWA_BLOCK
    tm = max(n * MOD_SUB for n in (1, 2, 3) if (ctx_len + t_len) % (n * MOD_SUB) == 0)
    tm_last = max(n * MOD_SUB for n in (1, 2, 3) if t_len % (n * MOD_SUB) == 0 and ctx_len % (n * MOD_SUB) == 0)

    cc = jnp.zeros((MOD_ROWS, d), F32).at[:b].set(c.astype(F32)).at[b].set(c_ctx.astype(F32))
    mods = _ada_call(cc, w_ada.astype(F32), b_ada.astype(F32))
    lb_all = _lb_call(hg_lb_logits)
    cos_t, sin_t = _rope_tables(t_len)

    n_head = 4 * GROUP_LANES + 16
    w_in_p = jnp.concatenate(
        [w_in[:, :, :n_head], jnp.zeros((depth, d, 128 - 16), w_in.dtype), w_in[:, :, n_head:]], axis=-1).astype(BF16)
    w_out_b = w_out.astype(BF16)
    w1_b = w_ff1.astype(BF16)
    w2_b = w_ff2.astype(BF16)

    xs = jnp.concatenate([ctx.astype(F32), x.astype(F32)], axis=1)
    for l in range(depth):
        last = l == depth - 1
        p_dn, p_sw, p_hg = _in_proj_call(xs, mods, norm1[l].astype(F32), w_in_p, l, ctx_len, tm)
        y_dn = _dn_call(p_dn, dn_conv[l], dn_A_log[l], dn_dt_bias[l], dn_norm[l], ctx_len)
        y_sw = _swa_call(p_sw, swa_sink[l], cos_t, sin_t, ctx_len)
        y_hg = _hg_call(p_hg, lb_all[:, l], hg_norm[l], ctx_len)
        xs = _out_mlp_call(xs, y_dn, y_sw, y_hg, mods, norm2[l].astype(F32), norm_f.astype(F32),
                           w_out_b, w1_b, w2_b, l, ctx_len, tm_last if last else tm, latent_only=last)
    return xs.astype(x.dtype)
```
